```python
import jax, jax.numpy as jnp
from jax import lax
import numpy as np

D_MODEL = 1024
BATCH = 8
SEQ = 2048
DEPTH = 2

N_MIXERS = 2
N_A_LAYERS = (DEPTH + 1) // 2
N_B_LAYERS = DEPTH // 2
EPS = 1e-6

HG_HEADS = 6
HG_KDIM = 128
HG_VDIM = 128
HG_CHUNK = 64
HG_F = HG_HEADS * HG_KDIM
HG_V = HG_HEADS * HG_VDIM

MLA_HEADS = 6
MLA_Q_LORA = 384
MLA_KV_LORA = 256
MLA_NOPE = 128
MLA_ROPE = 64
MLA_QK = MLA_NOPE + MLA_ROPE
MLA_VDIM = 128
ROPE_THETA = 10000.0
Q_BLOCK = 128

MEM_LEN = 256
MEM_HEADS = 4
MEM_HDIM = 64
MEM_W = MEM_HEADS * MEM_HDIM

MIX_W = 768
OUT_W = MIX_W + MEM_W
HG_IN_W = 2 * HG_F + 2 * HG_V + MEM_W
MLA_IN_W = MLA_Q_LORA + MLA_KV_LORA + MLA_ROPE + MEM_W

D_FF = 2816
CONV_W = 3

kernel_name = "hybrid_hgrn2_mla_memxattn_convffn"


def rms_norm(x, g):
    xf = x.astype(jnp.float32)
    y = xf * lax.rsqrt(jnp.mean(xf * xf, axis=-1, keepdims=True) + EPS)
    return (y * g.astype(jnp.float32)).astype(x.dtype)


def rope_tables(positions):
    half = MLA_ROPE // 2
    inv = ROPE_THETA ** (-jnp.arange(half, dtype=jnp.float32) / half)
    ang = positions.astype(jnp.float32)[..., None] * inv
    return jnp.cos(ang)[:, :, None, :], jnp.sin(ang)[:, :, None, :]


def apply_rope(x, cos, sin):
    half = MLA_ROPE // 2
    xf = x.astype(jnp.float32)
    x1, x2 = xf[..., :half], xf[..., half:]
    return jnp.concatenate([x1 * cos - x2 * sin, x1 * sin + x2 * cos], axis=-1).astype(x.dtype)


def hgrn2_mixer(z, lb, onorm_g):
    B, S, _ = z.shape
    C = HG_CHUNK
    N = S // C
    q, f, i, g = jnp.split(z, [HG_F, 2 * HG_F, 2 * HG_F + HG_V], axis=-1)
    q = jax.nn.silu(q.astype(jnp.float32)) * (HG_KDIM ** -0.5)
    fg = lb + (1.0 - lb) * jax.nn.sigmoid(f.astype(jnp.float32))
    k = 1.0 - fg
    logf = jnp.log(fg)
    v = i.astype(jnp.float32)

    def to_chunks(t, d):
        return t.reshape(B, N, C, HG_HEADS, d).transpose(1, 0, 3, 2, 4)

    qc, kc, lc = to_chunks(q, HG_KDIM), to_chunks(k, HG_KDIM), to_chunks(logf, HG_KDIM)
    vc = to_chunks(v, HG_VDIM)
    causal = jnp.tril(jnp.ones((C, C), dtype=bool))[:, :, None]

    def step(state, xs):
        q_, k_, v_, lf = xs
        b = jnp.cumsum(lf, axis=2)
        o_inter = jnp.einsum('bhtk,bhkv->bhtv', q_ * jnp.exp(b), state)
        diff = b[:, :, :, None, :] - b[:, :, None, :, :]
        decay = jnp.exp(jnp.where(causal, diff, -jnp.inf))
        attn = jnp.einsum('bhtk,bhtsk,bhsk->bhts', q_, decay, k_)
        o = o_inter + jnp.einsum('bhts,bhsv->bhtv', attn, v_)
        b_last = b[:, :, -1, :]
        state = state * jnp.exp(b_last)[..., None] + jnp.einsum(
            'bhsk,bhsv->bhkv', k_ * jnp.exp(b_last[:, :, None, :] - b), v_)
        return state, o

    s0 = jnp.zeros((B, HG_HEADS, HG_KDIM, HG_VDIM), jnp.float32)
    _, o = lax.scan(step, s0, (qc, kc, vc, lc))
    o = o.transpose(1, 0, 3, 2, 4).reshape(B, S, HG_HEADS, HG_VDIM)
    gate = g.astype(jnp.float32).reshape(B, S, HG_HEADS, HG_VDIM)
    o = rms_norm(o, onorm_g) * jax.nn.silu(gate)
    return o.reshape(B, S, HG_V).astype(z.dtype)


def causal_block_attention(q, k, v):
    B, S, H, Dq = q.shape
    nb = S // Q_BLOCK
    scale = Dq ** -0.5
    qb = q.reshape(B, nb, Q_BLOCK, H, Dq).transpose(1, 0, 2, 3, 4)
    kpos = jnp.arange(S)

    def one_block(args):
        qi, idx = args
        s = jnp.einsum('bqhd,bkhd->bhqk', qi, k).astype(jnp.float32) * scale
        qpos = idx * Q_BLOCK + jnp.arange(Q_BLOCK)
        s = jnp.where(kpos[None, :] <= qpos[:, None], s, -jnp.inf)
        p = jax.nn.softmax(s, axis=-1)
        return jnp.einsum('bhqk,bkhd->bqhd', p.astype(v.dtype), v)

    out = lax.map(one_block, (qb, jnp.arange(nb)))
    return out.transpose(1, 0, 2, 3, 4).reshape(B, S, H, v.shape[-1])


def mla_mixer(z, cos, sin, qa_g, kva_g, w_uq, w_ukv, qn_g, kn_g):
    B, S, _ = z.shape
    cq, ckv, kpe = jnp.split(z, [MLA_Q_LORA, MLA_Q_LORA + MLA_KV_LORA], axis=-1)
    q = (rms_norm(cq, qa_g) @ w_uq).reshape(B, S, MLA_HEADS, MLA_QK)
    kv = (rms_norm(ckv, kva_g) @ w_ukv).reshape(B, S, MLA_HEADS, MLA_NOPE + MLA_VDIM)
    k_nope, v = kv[..., :MLA_NOPE], kv[..., MLA_NOPE:]
    k_rope = jnp.broadcast_to(kpe[:, :, None, :], (B, S, MLA_HEADS, MLA_ROPE))
    k = jnp.concatenate([k_nope, k_rope], axis=-1)
    q = rms_norm(q, qn_g)
    k = rms_norm(k, kn_g)
    q = jnp.concatenate([q[..., :MLA_NOPE], apply_rope(q[..., MLA_NOPE:], cos, sin)], axis=-1)
    k = jnp.concatenate([k[..., :MLA_NOPE], apply_rope(k[..., MLA_NOPE:], cos, sin)], axis=-1)
    o = causal_block_attention(q, k, v)
    return o.reshape(B, S, MLA_HEADS * MLA_VDIM)


def memory_attention(qm, mem_n, w_mem_kv, qn_g, kn_g):
    B, S, _ = qm.shape
    M = mem_n.shape[1]
    kv = mem_n @ w_mem_kv
    q = rms_norm(qm.reshape(B, S, MEM_HEADS, MEM_HDIM), qn_g)
    k = rms_norm(kv[..., :MEM_W].reshape(B, M, MEM_HEADS, MEM_HDIM), kn_g)
    v = kv[..., MEM_W:].reshape(B, M, MEM_HEADS, MEM_HDIM)
    s = jnp.einsum('bshd,bmhd->bhsm', q, k).astype(jnp.float32) * (MEM_HDIM ** -0.5)
    p = jax.nn.softmax(s, axis=-1)
    o = jnp.einsum('bhsm,bmhd->bshd', p.astype(v.dtype), v)
    return o.reshape(B, S, MEM_W)


def conv_ffn(h, w_up, conv_w, conv_b, w_down):
    u = h @ w_up
    u = lax.conv_general_dilated(
        u, conv_w[:, None, :].astype(u.dtype), window_strides=(1,),
        padding=[(CONV_W - 1, 0)], dimension_numbers=('NWC', 'WIO', 'NWC'),
        feature_group_count=u.shape[-1]) + conv_b
    a, b = u[..., :D_FF], u[..., D_FF:]
    return (jax.nn.silu(a) * b) @ w_down


def setup_inputs(seed: int = 0) -> dict:
    key = jax.random.key(seed)
    ks = jax.random.split(key, 32)
    nrm = lambda k, shape, scale: jax.random.normal(k, shape, jnp.float32) * scale
    gain = lambda k, shape: 1.0 + 0.02 * jax.random.normal(k, shape, jnp.float32)
    offs = jax.random.randint(ks[2], (BATCH, 1), 0, 1024, dtype=jnp.int32)
    positions = offs + jnp.arange(SEQ, dtype=jnp.int32)[None, :]
    return {
        "x": nrm(ks[0], (BATCH, SEQ, D_MODEL), 1.0),
        "mem": nrm(ks[1], (BATCH, MEM_LEN, D_MODEL), 1.0),
        "positions": positions,
        "mix_norm_g": gain(ks[3], (DEPTH, D_MODEL)),
        "ffn_norm_g": gain(ks[4], (DEPTH, D_MODEL)),
        "mem_norm_g": gain(ks[5], (DEPTH, D_MODEL)),
        "w_mem_kv": nrm(ks[6], (DEPTH, D_MODEL, 2 * MEM_W), D_MODEL ** -0.5),
        "mem_q_norm_g": gain(ks[7], (DEPTH, MEM_HDIM)),
        "mem_k_norm_g": gain(ks[8], (DEPTH, MEM_HDIM)),
        "w_out": nrm(ks[9], (DEPTH, OUT_W, D_MODEL), OUT_W ** -0.5),
        "w_up": nrm(ks[10], (DEPTH, D_MODEL, 2 * D_FF), D_MODEL ** -0.5),
        "conv_w": nrm(ks[11], (DEPTH, CONV_W, 2 * D_FF), CONV_W ** -0.5),
        "conv_b": nrm(ks[12], (DEPTH, 2 * D_FF), 0.02),
        "w_down": nrm(ks[13], (DEPTH, D_FF, D_MODEL), D_FF ** -0.5),
        "hg_w_in": nrm(ks[14], (N_A_LAYERS, D_MODEL, HG_IN_W), D_MODEL ** -0.5),
        "hg_lb_logits": nrm(ks[15], (DEPTH + 1, HG_F), 0.5),
        "hg_out_norm_g": gain(ks[16], (N_A_LAYERS, HG_VDIM)),
        "mla_w_in": nrm(ks[17], (N_B_LAYERS, D_MODEL, MLA_IN_W), D_MODEL ** -0.5),
        "mla_qa_norm_g": gain(ks[18], (N_B_LAYERS, MLA_Q_LORA)),
        "mla_kva_norm_g": gain(ks[19], (N_B_LAYERS, MLA_KV_LORA)),
        "mla_w_uq": nrm(ks[20], (N_B_LAYERS, MLA_Q_LORA, MLA_HEADS * MLA_QK), MLA_Q_LORA ** -0.5),
        "mla_w_ukv": nrm(ks[21], (N_B_LAYERS, MLA_KV_LORA, MLA_HEADS * (MLA_NOPE + MLA_VDIM)), MLA_KV_LORA ** -0.5),
        "mla_q_norm_g": gain(ks[22], (N_B_LAYERS, MLA_QK)),
        "mla_k_norm_g": gain(ks[23], (N_B_LAYERS, MLA_QK)),
    }


def reference(x, mem, positions, mix_norm_g, ffn_norm_g, mem_norm_g, w_mem_kv,
              mem_q_norm_g, mem_k_norm_g, w_out, w_up, conv_w, conv_b, w_down,
              hg_w_in, hg_lb_logits, hg_out_norm_g, mla_w_in, mla_qa_norm_g,
              mla_kva_norm_g, mla_w_uq, mla_w_ukv, mla_q_norm_g, mla_k_norm_g):
    cos, sin = rope_tables(positions)
    lower_bounds = jnp.cumsum(jax.nn.softmax(hg_lb_logits.astype(jnp.float32), axis=0), axis=0)
    for layer in range(DEPTH):
        h = rms_norm(x, mix_norm_g[layer])
        j = layer // N_MIXERS
        if layer % N_MIXERS == 0:
            z = h @ hg_w_in[j]
            y = hgrn2_mixer(z[..., :-MEM_W], lower_bounds[layer], hg_out_norm_g[j])
        else:
            z = h @ mla_w_in[j]
            y = mla_mixer(z[..., :-MEM_W], cos, sin, mla_qa_norm_g[j], mla_kva_norm_g[j],
                          mla_w_uq[j], mla_w_ukv[j], mla_q_norm_g[j], mla_k_norm_g[j])
        m = memory_attention(z[..., -MEM_W:], rms_norm(mem, mem_norm_g[layer]), w_mem_kv[layer],
                             mem_q_norm_g[layer], mem_k_norm_g[layer])
        x = x + jnp.concatenate([y, m], axis=-1) @ w_out[layer]
        x = x + conv_ffn(rms_norm(x, ffn_norm_g[layer]), w_up[layer], conv_w[layer],
                         conv_b[layer], w_down[layer])
    return x
```

```python
import functools

import numpy as np
import jax
import jax.numpy as jnp
from jax import lax
from jax.experimental import pallas as pl
from jax.experimental.pallas import tpu as pltpu

F32 = jnp.float32
BF16 = jnp.bfloat16
EPS = 1e-6

HG_HEADS = 6
HG_KDIM = 128
HG_VDIM = 128
HG_F = HG_HEADS * HG_KDIM
MLA_HEADS = 6
MLA_Q_LORA = 384
MLA_KV_LORA = 256
MLA_NOPE = 128
MLA_ROPE = 64
MLA_QK = MLA_NOPE + MLA_ROPE
MLA_VDIM = 128
MLA_PAD = 256
ROPE_THETA = 10000.0
MEM_HEADS = 4
MEM_HDIM = 64
MEM_W = MEM_HEADS * MEM_HDIM
MIX_W = 768
D_FF = 2816
CONV_W = 3

LANES = 128
SUBLANES = 8
VMEM_LIMIT = 56 * 1024 * 1024

HG_CHUNK = 128
HG_LEVELS = 7
FFN_CHUNK = 256
NEG_BIG = -1e30


def _dot(a, b):
    return jnp.dot(a, b, preferred_element_type=F32)


def _dot_nt(a, b):
    return lax.dot_general(a, b, (((1,), (1,)), ((), ())), preferred_element_type=F32)


def _dot_tn(a, b):
    return lax.dot_general(a, b, (((0,), (0,)), ((), ())), preferred_element_type=F32)


def _rms(x, g):
    ms = jnp.mean(x * x, axis=-1, keepdims=True)
    return x * lax.rsqrt(ms + EPS) * g


def _sigmoid(x):
    return 1.0 / (1.0 + jnp.exp(-x))


def _split2(x):
    hi = x.astype(BF16)
    lo = (x - hi.astype(F32)).astype(BF16)
    return hi, lo


def _split3(x):
    hi = x.astype(BF16)
    r = x - hi.astype(F32)
    mid = r.astype(BF16)
    lo = (r - mid.astype(F32)).astype(BF16)
    return hi, mid, lo


def _const_spec(shape):
    nd = len(shape)
    return pl.BlockSpec(shape, lambda *_: (0,) * nd, pipeline_mode=pl.Buffered(1))


def _params(sem):
    return pltpu.CompilerParams(dimension_semantics=sem, vmem_limit_bytes=VMEM_LIMIT)


def _hg_inproj_kernel(x_ref, g_ref, w_ref, lbl_ref, qs_ref, lf_ref, kk_ref, v_ref, sg_ref, qm_ref):
    h = _rms(x_ref[...], g_ref[...]).astype(BF16)
    l = lbl_ref[...]
    e = jnp.exp(l - jnp.max(l, axis=0, keepdims=True))
    lb = e[0:1, :] / jnp.sum(e, axis=0, keepdims=True)
    q = _dot(h, w_ref[:, 0:HG_F])
    qs_ref[...] = q * _sigmoid(q) * (HG_KDIM ** -0.5)
    f = _dot(h, w_ref[:, HG_F:2 * HG_F])
    fg = lb + (1.0 - lb) * _sigmoid(f)
    lf_ref[...] = jnp.log(fg)
    kk_ref[...] = 1.0 - fg
    v_ref[...] = _dot(h, w_ref[:, 2 * HG_F:3 * HG_F]).astype(BF16)
    gate = _dot(h, w_ref[:, 3 * HG_F:4 * HG_F])
    sg_ref[...] = gate * _sigmoid(gate)
    qm_ref[...] = _dot(h, w_ref[:, 4 * HG_F:4 * HG_F + MEM_W])


def _hg_inproj(x2, g, w, lb_logits, tm):
    n, d = x2.shape
    row = lambda i: (i, 0)
    out768 = pl.BlockSpec((tm, HG_F), row)
    return pl.pallas_call(
        _hg_inproj_kernel,
        grid=(n // tm,),
        in_specs=[pl.BlockSpec((tm, d), row), _const_spec((1, d)), _const_spec(w.shape),
                  _const_spec(lb_logits.shape)],
        out_specs=[out768, out768, out768, out768, out768, pl.BlockSpec((tm, MEM_W), row)],
        out_shape=[jax.ShapeDtypeStruct((n, HG_F), F32), jax.ShapeDtypeStruct((n, HG_F), F32),
                   jax.ShapeDtypeStruct((n, HG_F), F32), jax.ShapeDtypeStruct((n, HG_F), BF16),
                   jax.ShapeDtypeStruct((n, HG_F), F32), jax.ShapeDtypeStruct((n, MEM_W), F32)],
        compiler_params=_params(("arbitrary",)),
    )(x2, g, w, lb_logits)


def _memkv_kernel(mem_ref, g_ref, w_ref, kg_ref, grp_ref, k_ref, v_ref):
    mn = _rms(mem_ref[...], g_ref[...]).astype(BF16)
    kv = _dot(mn, w_ref[...])
    k = kv[:, :MEM_W]
    hi, lo = _split2(k * k)
    ms = _dot(hi, grp_ref[...]) + _dot(lo, grp_ref[...])
    k_ref[...] = (k * lax.rsqrt(ms + EPS) * kg_ref[...]).astype(BF16)
    v_ref[...] = kv[:, MEM_W:].astype(BF16)


def _memkv(mem, g, w, kg, grp):
    depth = g.shape[0]
    b, m, d = mem.shape
    spec_kv = pl.BlockSpec((None, None, m, MEM_W), lambda l, i: (l, i, 0, 0))
    return pl.pallas_call(
        _memkv_kernel,
        grid=(depth, b),
        in_specs=[pl.BlockSpec((None, m, d), lambda l, i: (i, 0, 0)),
                  pl.BlockSpec((None, 1, d), lambda l, i: (l, 0, 0)),
                  pl.BlockSpec((None, d, 2 * MEM_W), lambda l, i: (l, 0, 0)),
                  pl.BlockSpec((None, 1, MEM_W), lambda l, i: (l, 0, 0)),
                  _const_spec(grp.shape)],
        out_specs=[spec_kv, spec_kv],
        out_shape=[jax.ShapeDtypeStruct((depth, b, m, MEM_W), BF16)] * 2,
        compiler_params=_params(("arbitrary", "arbitrary")),
    )(mem, g, w, kg, grp)


def _hg_tables():
    C = HG_CHUNK
    mats = []
    t = np.arange(C)[:, None]
    u = np.arange(C)[None, :]
    for l in range(HG_LEVELS):
        c = C >> (l + 1)
        mid = (t // (2 * c)) * (2 * c) + c
        upper = t >= mid
        mats.append(np.where(upper, (u >= mid) & (u <= t), (u > t) & (u < mid)))
    mats.append(u <= t)
    mats.append(u > t)
    m = np.concatenate(mats, axis=0).astype(np.float32)
    s = u
    x = t ^ s
    p = np.floor(np.log2(np.maximum(x, 1))).astype(np.int32)
    lev = np.where(t > s, HG_LEVELS - 1 - p, np.where(t == s, HG_LEVELS, -1)).astype(np.int32)
    return m, lev


def _hgrn_kernel(qs_ref, lf_ref, kk_ref, v_ref, sg_ref, m_ref, lev_ref, og_ref, y_ref, st_ref):
    C = HG_CHUNK

    @pl.when(pl.program_id(1) == 0)
    def _():
        st_ref[...] = jnp.zeros_like(st_ref)

    lev = lev_ref[...]
    mstack = m_ref[...]
    for h in range(HG_HEADS):
        cols = slice(h * HG_KDIM, (h + 1) * HG_KDIM)
        q = qs_ref[:, cols]
        k = kk_ref[:, cols]
        v = v_ref[:, cols]
        hi, mid, lo = _split3(lf_ref[:, cols])
        e_all = jnp.exp(_dot(mstack, hi) + _dot(mstack, mid) + _dot(mstack, lo))
        a = jnp.where(lev == HG_LEVELS, _dot_nt(q.astype(BF16), k.astype(BF16)), 0.0)
        for l in range(HG_LEVELS):
            e = e_all[l * C:(l + 1) * C, :]
            a_l = _dot_nt((q * e).astype(BF16), (k * e).astype(BF16))
            a = jnp.where(lev == l, a_l, a)
        e_pre = e_all[HG_LEVELS * C:(HG_LEVELS + 1) * C, :]
        e_suf = e_all[(HG_LEVELS + 1) * C:(HG_LEVELS + 2) * C, :]
        st = st_ref[h]
        o = _dot_nt((q * e_pre).astype(BF16), st.astype(BF16)) + _dot(a.astype(BF16), v)
        st_ref[h] = st * e_pre[C - 1:C, :] + _dot_tn(v, (k * e_suf).astype(BF16))
        y_ref[:, cols] = (_rms(o, og_ref[...]) * sg_ref[:, cols]).astype(BF16)


def _hgrn(qs, lf, kk, v, sg, mstack, lev, og, batch, seq):
    n = qs.shape[0]
    nc = seq // HG_CHUNK
    row = lambda b, c: (b * nc + c, 0)
    spec = pl.BlockSpec((HG_CHUNK, HG_F), row)
    return pl.pallas_call(
        _hgrn_kernel,
        grid=(batch, nc),
        in_specs=[spec, spec, spec, spec, spec, _const_spec(mstack.shape), _const_spec(lev.shape),
                  _const_spec(og.shape)],
        out_specs=spec,
        out_shape=jax.ShapeDtypeStruct((n, HG_F), BF16),
        scratch_shapes=[pltpu.VMEM((HG_HEADS, HG_VDIM, HG_KDIM), F32)],
        compiler_params=_params(("arbitrary", "arbitrary")),
    )(qs, lf, kk, v, sg, mstack, lev, og)


def _outproj_kernel(x_ref, y_ref, qm_ref, km_ref, vm_ref, qg_ref, grp_ref, w_ref, o_ref):
    q = qm_ref[...]
    hi, lo = _split2(q * q)
    ms = _dot(hi, grp_ref[...]) + _dot(lo, grp_ref[...])
    qn = q * lax.rsqrt(ms + EPS) * (qg_ref[...] * (MEM_HDIM ** -0.5))
    km = km_ref[...]
    vm = vm_ref[...]
    lane = lax.broadcasted_iota(jnp.int32, qn.shape, 1)
    m = jnp.zeros(qn.shape, F32)
    for h in range(MEM_HEADS):
        in_head = (lane >= h * MEM_HDIM) & (lane < (h + 1) * MEM_HDIM)
        s = _dot_nt(jnp.where(in_head, qn, 0.0).astype(BF16), km)
        p = jnp.exp(s - jnp.max(s, axis=-1, keepdims=True))
        o = _dot(p.astype(BF16), vm) / jnp.sum(p, axis=-1, keepdims=True)
        m = jnp.where(in_head, o, m)
    o_ref[...] = (x_ref[...] + _dot(y_ref[...], w_ref[0:MIX_W, :])
                  + _dot(m.astype(BF16), w_ref[MIX_W:MIX_W + MEM_W, :]))


def _outproj(x2, y, qm, kmem, vmem, layer, qg, grp, w, batch, seq, tm):
    n, d = x2.shape
    nt = seq // tm
    row = lambda b, t: (b * nt + t, 0)
    mem_spec = pl.BlockSpec((None, None) + kmem.shape[2:], lambda b, t: (layer, b, 0, 0))
    return pl.pallas_call(
        _outproj_kernel,
        grid=(batch, nt),
        in_specs=[pl.BlockSpec((tm, d), row), pl.BlockSpec((tm, MIX_W), row),
                  pl.BlockSpec((tm, MEM_W), row), mem_spec, mem_spec,
                  _const_spec(qg.shape), _const_spec(grp.shape), _const_spec(w.shape)],
        out_specs=pl.BlockSpec((tm, d), row),
        out_shape=jax.ShapeDtypeStruct((n, d), F32),
        compiler_params=_params(("arbitrary", "arbitrary")),
    )(x2, y, qm, kmem, vmem, qg, grp, w)


def _shift_rows(u, prev, d):
    r = pltpu.roll(u, d, axis=0)
    pr = pltpu.roll(prev, d, axis=0)
    sub = lax.broadcasted_iota(jnp.int32, pr.shape, 0)
    head = jnp.where(sub < d, pr, r[0:SUBLANES, :])
    return jnp.concatenate([head, r[SUBLANES:, :]], axis=0)


def _ffn_kernel(x_ref, g_ref, wup_ref, cw_ref, cb_ref, wdn_ref, o_ref, carry_ref, act_ref):
    tm = x_ref.shape[0]

    @pl.when(pl.program_id(1) == 0)
    def _():
        carry_ref[...] = jnp.zeros_like(carry_ref)

    x = x_ref[...]
    h = _rms(x, g_ref[...]).astype(BF16)

    def conv_slice(c0):
        cols = slice(c0, c0 + FFN_CHUNK)
        u = _dot(h, wup_ref[:, cols])
        prev = carry_ref[:, cols]
        carry_ref[:, cols] = u[tm - SUBLANES:, :]
        return (cw_ref[2:3, cols] * u + cw_ref[1:2, cols] * _shift_rows(u, prev, 1)
                + cw_ref[0:1, cols] * _shift_rows(u, prev, 2) + cb_ref[:, cols])

    for c in range(D_FF // FFN_CHUNK):
        a = conv_slice(c * FFN_CHUNK)
        b = conv_slice(D_FF + c * FFN_CHUNK)
        act_ref[:, c * FFN_CHUNK:(c + 1) * FFN_CHUNK] = (a * _sigmoid(a) * b).astype(BF16)
    o_ref[...] = x + _dot(act_ref[...], wdn_ref[...])


def _ffn(x2, g, wup, cw, cb, wdn, batch, seq, tm):
    n, d = x2.shape
    nt = seq // tm
    row = lambda b, t: (b * nt + t, 0)
    return pl.pallas_call(
        _ffn_kernel,
        grid=(batch, nt),
        in_specs=[pl.BlockSpec((tm, d), row), _const_spec(g.shape), _const_spec(wup.shape),
                  _const_spec(cw.shape), _const_spec(cb.shape), _const_spec(wdn.shape)],
        out_specs=pl.BlockSpec((tm, d), row),
        out_shape=jax.ShapeDtypeStruct((n, d), F32),
        scratch_shapes=[pltpu.VMEM((SUBLANES, 2 * D_FF), F32), pltpu.VMEM((tm, D_FF), BF16)],
        compiler_params=_params(("arbitrary", "arbitrary")),
    )(x2, g, wup, cw, cb, wdn)


def _mla_prep_kernel(x_ref, pos_ref, g_ref, win_ref, inv_ref, qag_ref, kvag_ref, wuq_ref, wukv_ref,
                     qgn_ref, qgc_ref, qgs_ref, kgn_ref, kgr_ref, sgn_ref,
                     q_ref, k_ref, v_ref, qm_ref):
    h = _rms(x_ref[...], g_ref[...]).astype(BF16)
    z = _dot(h, win_ref[...])
    cq = z[:, 0:MLA_Q_LORA]
    ckv = z[:, MLA_Q_LORA:MLA_Q_LORA + MLA_KV_LORA]
    kpe2 = z[:, MLA_Q_LORA + MLA_KV_LORA:MLA_Q_LORA + MLA_KV_LORA + LANES]
    qm_ref[...] = z[:, MLA_Q_LORA + MLA_KV_LORA + LANES:]

    ang = pos_ref[...].astype(F32) * inv_ref[...]
    cos = jnp.cos(ang)
    sin = jnp.sin(ang)

    qf = _dot(_rms(cq, qag_ref[...]).astype(BF16), wuq_ref[...])
    tq = cos * qgc_ref[...] + sin * qgs_ref[...]
    for hd in range(MLA_HEADS):
        qn = qf[:, hd * MLA_PAD:hd * MLA_PAD + MLA_NOPE]
        qr = qf[:, hd * MLA_PAD + MLA_NOPE:(hd + 1) * MLA_PAD]
        ss = jnp.sum(qn * qn, axis=-1, keepdims=True) + 0.5 * jnp.sum(qr * qr, axis=-1, keepdims=True)
        rs = lax.rsqrt(ss * (1.0 / MLA_QK) + EPS) * (MLA_QK ** -0.5)
        q_ref[:, hd * MLA_PAD:hd * MLA_PAD + MLA_NOPE] = (qn * rs * qgn_ref[...]).astype(BF16)
        q_ref[:, hd * MLA_PAD + MLA_NOPE:(hd + 1) * MLA_PAD] = (qr * rs * tq).astype(BF16)

    kvf = _dot(_rms(ckv, kvag_ref[...]).astype(BF16), wukv_ref[...])
    pk = kpe2 * kgr_ref[...]
    kr = pk * cos + pltpu.roll(pk, MLA_ROPE // 2, axis=1) * (sin * sgn_ref[...])
    sk = 0.5 * jnp.sum(kpe2 * kpe2, axis=-1, keepdims=True)
    for hd in range(MLA_HEADS):
        kn = kvf[:, hd * MLA_PAD:hd * MLA_PAD + MLA_NOPE]
        ss = jnp.sum(kn * kn, axis=-1, keepdims=True) + sk
        rs = lax.rsqrt(ss * (1.0 / MLA_QK) + EPS)
        k_ref[:, hd * MLA_PAD:hd * MLA_PAD + MLA_NOPE] = (kn * rs * kgn_ref[...]).astype(BF16)
        k_ref[:, hd * MLA_PAD + MLA_NOPE:(hd + 1) * MLA_PAD] = (kr * rs).astype(BF16)
        v_ref[:, hd * MLA_VDIM:(hd + 1) * MLA_VDIM] = kvf[:, hd * MLA_PAD + MLA_NOPE:(hd + 1) * MLA_PAD].astype(BF16)


def _mla_prep(x2, pos, g, win, inv, qag, kvag, wuq, wukv, tabs, tm):
    n, d = x2.shape
    row = lambda i: (i, 0)
    consts = [g, win, inv, qag, kvag, wuq, wukv] + list(tabs)
    return pl.pallas_call(
        _mla_prep_kernel,
        grid=(n // tm,),
        in_specs=[pl.BlockSpec((tm, d), row), pl.BlockSpec((tm, 1), row)]
                 + [_const_spec(c.shape) for c in consts],
        out_specs=[pl.BlockSpec((tm, MLA_HEADS * MLA_PAD), row), pl.BlockSpec((tm, MLA_HEADS * MLA_PAD), row),
                   pl.BlockSpec((tm, MLA_HEADS * MLA_VDIM), row), pl.BlockSpec((tm, MEM_W), row)],
        out_shape=[jax.ShapeDtypeStruct((n, MLA_HEADS * MLA_PAD), BF16),
                   jax.ShapeDtypeStruct((n, MLA_HEADS * MLA_PAD), BF16),
                   jax.ShapeDtypeStruct((n, MLA_HEADS * MLA_VDIM), BF16),
                   jax.ShapeDtypeStruct((n, MEM_W), F32)],
        compiler_params=_params(("arbitrary",)),
    )(x2, pos, *consts)


def _flash_kernel(q_ref, k_ref, v_ref, o_ref):
    tq = q_ref.shape[0]
    qi = pl.program_id(2)
    q = q_ref[...]

    def block(j, carry, masked):
        m, l, acc = carry
        start = pl.multiple_of(j * tq, tq)
        s = _dot_nt(q, k_ref[pl.ds(start, tq), :])
        if masked:
            r = lax.broadcasted_iota(jnp.int32, s.shape, 0)
            c = lax.broadcasted_iota(jnp.int32, s.shape, 1)
            s = jnp.where(c <= r, s, -jnp.inf)
        m_new = jnp.maximum(m, jnp.max(s, axis=-1, keepdims=True))
        p = jnp.exp(s - m_new)
        alpha = jnp.exp(m - m_new)
        l = alpha * l + jnp.sum(p, axis=-1, keepdims=True)
        acc = alpha * acc + _dot(p.astype(BF16), v_ref[pl.ds(start, tq), :])
        return m_new, l, acc

    init = (jnp.full((tq, 1), NEG_BIG, F32), jnp.zeros((tq, 1), F32), jnp.zeros((tq, MLA_VDIM), F32))
    carry = lax.fori_loop(0, qi, lambda j, c: block(j, c, False), init)
    m, l, acc = block(qi, carry, True)
    o_ref[...] = (acc / l).astype(BF16)


def _flash(q, k, v, batch, seq, tq):
    q3 = q.reshape(batch, seq, MLA_HEADS * MLA_PAD)
    k3 = k.reshape(batch, seq, MLA_HEADS * MLA_PAD)
    v3 = v.reshape(batch, seq, MLA_HEADS * MLA_VDIM)
    out = pl.pallas_call(
        _flash_kernel,
        grid=(batch, MLA_HEADS, seq // tq),
        in_specs=[pl.BlockSpec((None, tq, MLA_PAD), lambda b, h, i: (b, i, h)),
                  pl.BlockSpec((None, seq, MLA_PAD), lambda b, h, i: (b, 0, h)),
                  pl.BlockSpec((None, seq, MLA_VDIM), lambda b, h, i: (b, 0, h))],
        out_specs=pl.BlockSpec((None, tq, MLA_VDIM), lambda b, h, i: (b, i, h)),
        out_shape=jax.ShapeDtypeStruct((batch, seq, MLA_HEADS * MLA_VDIM), BF16),
        compiler_params=_params(("arbitrary", "arbitrary", "arbitrary")),
    )(q3, k3, v3)
    return out.reshape(batch * seq, MLA_HEADS * MLA_VDIM)


def _mla_layouts(w_in, w_uq, q_norm_g, k_norm_g):
    half = MLA_ROPE // 2
    c0 = MLA_Q_LORA + MLA_KV_LORA
    kpe = w_in[:, c0:c0 + MLA_ROPE]
    win = jnp.concatenate([w_in[:, :c0], kpe, kpe, w_in[:, c0 + MLA_ROPE:]], axis=1)
    wq = w_uq.reshape(MLA_Q_LORA, MLA_HEADS, MLA_QK)
    x1 = wq[:, :, MLA_NOPE:MLA_NOPE + half]
    x2 = wq[:, :, MLA_NOPE + half:]
    wuq = jnp.concatenate([wq[:, :, :MLA_NOPE], x1, x2, x2, x1], axis=2).reshape(MLA_Q_LORA, MLA_HEADS * MLA_PAD)
    z = jnp.zeros((half,), F32)
    g1, g2 = q_norm_g[MLA_NOPE:MLA_NOPE + half], q_norm_g[MLA_NOPE + half:]
    qgn = q_norm_g[:MLA_NOPE][None, :]
    qgc = jnp.concatenate([g1, g2, z, z])[None, :]
    qgs = jnp.concatenate([z, z, -g2, g1])[None, :]
    kgn = k_norm_g[:MLA_NOPE][None, :]
    kgr = jnp.concatenate([k_norm_g[MLA_NOPE:], k_norm_g[MLA_NOPE:]])[None, :]
    one = jnp.ones((half,), F32)
    sgn = jnp.concatenate([-one, one, -one, one])[None, :]
    return win, wuq, (qgn, qgc, qgs, kgn, kgr, sgn)


def kernel(x, mem, positions, mix_norm_g, ffn_norm_g, mem_norm_g, w_mem_kv, mem_q_norm_g, mem_k_norm_g, w_out, w_up, conv_w, conv_b, w_down, hg_w_in, hg_lb_logits, hg_out_norm_g, mla_w_in, mla_qa_norm_g, mla_kva_norm_g, mla_w_uq, mla_w_ukv, mla_q_norm_g, mla_k_norm_g):
    batch, seq, d = x.shape
    n = batch * seq
    tm = min(256, seq)
    depth = mix_norm_g.shape[0]
    x2 = x.reshape(n, d)

    grp = jnp.asarray(np.kron(np.eye(MEM_HEADS), np.full((MEM_HDIM, MEM_HDIM), 1.0 / MEM_HDIM)), BF16)
    mstack_np, lev_np = _hg_tables()
    mstack = jnp.asarray(mstack_np, BF16)
    lev = jnp.asarray(lev_np)

    kmem, vmem = _memkv(mem, mem_norm_g[:, None, :], w_mem_kv.astype(BF16),
                        jnp.tile(mem_k_norm_g, (1, MEM_HEADS))[:, None, :], grp)
    qg = jnp.tile(mem_q_norm_g, (1, MEM_HEADS))

    def finish_layer(xin, y, qm, layer):
        x1 = _outproj(xin, y, qm, kmem, vmem, layer, qg[layer][None, :], grp,
                      w_out[layer].astype(BF16), batch, seq, tm)
        return _ffn(x1, ffn_norm_g[layer][None, :], w_up[layer].astype(BF16), conv_w[layer],
                    conv_b[layer][None, :], w_down[layer].astype(BF16), batch, seq, tm)

    qs, lf, kk, v, sg, qm = _hg_inproj(x2, mix_norm_g[0][None, :], hg_w_in[0].astype(BF16),
                                        hg_lb_logits, tm)
    y = _hgrn(qs, lf, kk, v, sg, mstack, lev, hg_out_norm_g[0][None, :], batch, seq)
    x2 = finish_layer(x2, y, qm, 0)

    win, wuq, tabs = _mla_layouts(mla_w_in[0], mla_w_uq[0], mla_q_norm_g[0], mla_k_norm_g[0])
    half = MLA_ROPE // 2
    inv = ROPE_THETA ** (-jnp.arange(half, dtype=F32) / half)
    inv = jnp.tile(inv, LANES // half)[None, :]
    q, k, v, qm = _mla_prep(x2, positions.reshape(n, 1), mix_norm_g[1][None, :], win.astype(BF16), inv,
                            mla_qa_norm_g[0][None, :], mla_kva_norm_g[0][None, :], wuq.astype(BF16),
                            mla_w_ukv[0].astype(BF16), tabs, tm)
    y = _flash(q, k, v, batch, seq, tm)
    x2 = finish_layer(x2, y, qm, 1)
    return x2.reshape(batch, seq, d)
```

```python
import functools

import numpy as np
import jax
import jax.numpy as jnp
from jax import lax
from jax.experimental import pallas as pl
from jax.experimental.pallas import tpu as pltpu

F32 = jnp.float32
BF16 = jnp.bfloat16
EPS = 1e-6

HG_HEADS = 6
HG_KDIM = 128
HG_VDIM = 128
HG_F = HG_HEADS * HG_KDIM
MLA_HEADS = 6
MLA_Q_LORA = 384
MLA_KV_LORA = 256
MLA_NOPE = 128
MLA_ROPE = 64
MLA_QK = MLA_NOPE + MLA_ROPE
MLA_VDIM = 128
MLA_PAD = 256
ROPE_THETA = 10000.0
MEM_HEADS = 4
MEM_HDIM = 64
MEM_W = MEM_HEADS * MEM_HDIM
MIX_W = 768
D_FF = 2816
CONV_W = 3

LANES = 128
SUBLANES = 8
VMEM_LIMIT = 56 * 1024 * 1024

HG_CHUNK = 128
HG_LEVELS = 7
HG_BIG_LEVELS = 4
FFN_CHUNK = 256
NEG_BIG = -1e30
LOG2E = 1.4426950408889634
FLASH_HEADS = 2
FLASH_TQ = 512


def _dot(a, b):
    return jnp.dot(a, b, preferred_element_type=F32)


def _dot_nt(a, b):
    return lax.dot_general(a, b, (((1,), (1,)), ((), ())), preferred_element_type=F32)


def _dot_tn(a, b):
    return lax.dot_general(a, b, (((0,), (0,)), ((), ())), preferred_element_type=F32)


def _rms(x, g):
    ms = jnp.mean(x * x, axis=-1, keepdims=True)
    return x * lax.rsqrt(ms + EPS) * g


def _sigmoid(x):
    return 1.0 / (1.0 + jnp.exp(-x))


def _split2(x):
    hi = x.astype(BF16)
    lo = (x - hi.astype(F32)).astype(BF16)
    return hi, lo


def _split3(x):
    hi = x.astype(BF16)
    r = x - hi.astype(F32)
    mid = r.astype(BF16)
    lo = (r - mid.astype(F32)).astype(BF16)
    return hi, mid, lo


def _const_spec(shape):
    nd = len(shape)
    return pl.BlockSpec(shape, lambda *_: (0,) * nd, pipeline_mode=pl.Buffered(1))


def _params(sem):
    return pltpu.CompilerParams(dimension_semantics=sem, vmem_limit_bytes=VMEM_LIMIT)


def _hg_inproj_kernel(x_ref, g_ref, w_ref, lbl_ref, qs_ref, lf_ref, kk_ref, v_ref, sg_ref, qm_ref):
    h = _rms(x_ref[...], g_ref[...]).astype(BF16)
    l = lbl_ref[...]
    e = jnp.exp(l - jnp.max(l, axis=0, keepdims=True))
    lb = e[0:1, :] / jnp.sum(e, axis=0, keepdims=True)
    q = _dot(h, w_ref[:, 0:HG_F])
    qs_ref[...] = q * _sigmoid(q) * (HG_KDIM ** -0.5)
    f = _dot(h, w_ref[:, HG_F:2 * HG_F])
    fg = lb + (1.0 - lb) * _sigmoid(f)
    lf_ref[...] = jnp.log2(fg)
    kk_ref[...] = 1.0 - fg
    v_ref[...] = _dot(h, w_ref[:, 2 * HG_F:3 * HG_F]).astype(BF16)
    gate = _dot(h, w_ref[:, 3 * HG_F:4 * HG_F])
    sg_ref[...] = gate * _sigmoid(gate)
    qm_ref[...] = _dot(h, w_ref[:, 4 * HG_F:4 * HG_F + MEM_W])


def _hg_inproj(x2, g, w, lb_logits, tm):
    n, d = x2.shape
    row = lambda i: (i, 0)
    out768 = pl.BlockSpec((tm, HG_F), row)
    return pl.pallas_call(
        _hg_inproj_kernel,
        grid=(n // tm,),
        in_specs=[pl.BlockSpec((tm, d), row), _const_spec((1, d)), _const_spec(w.shape),
                  _const_spec(lb_logits.shape)],
        out_specs=[out768, out768, out768, out768, out768, pl.BlockSpec((tm, MEM_W), row)],
        out_shape=[jax.ShapeDtypeStruct((n, HG_F), F32), jax.ShapeDtypeStruct((n, HG_F), F32),
                   jax.ShapeDtypeStruct((n, HG_F), F32), jax.ShapeDtypeStruct((n, HG_F), BF16),
                   jax.ShapeDtypeStruct((n, HG_F), F32), jax.ShapeDtypeStruct((n, MEM_W), F32)],
        compiler_params=_params(("arbitrary",)),
    )(x2, g, w, lb_logits)


def _memkv_kernel(mem_ref, g_ref, w_ref, kg_ref, grp_ref, k_ref, v_ref):
    mn = _rms(mem_ref[...], g_ref[...]).astype(BF16)
    kv = _dot(mn, w_ref[...])
    k = kv[:, :MEM_W]
    hi, lo = _split2(k * k)
    ms = _dot(hi, grp_ref[...]) + _dot(lo, grp_ref[...])
    k_ref[...] = (k * lax.rsqrt(ms + EPS) * kg_ref[...]).astype(BF16)
    v_ref[...] = kv[:, MEM_W:].astype(BF16)


def _memkv(mem, g, w, kg, grp):
    depth = g.shape[0]
    b, m, d = mem.shape
    spec_kv = pl.BlockSpec((None, None, m, MEM_W), lambda l, i: (l, i, 0, 0))
    return pl.pallas_call(
        _memkv_kernel,
        grid=(depth, b),
        in_specs=[pl.BlockSpec((None, m, d), lambda l, i: (i, 0, 0)),
                  pl.BlockSpec((None, 1, d), lambda l, i: (l, 0, 0)),
                  pl.BlockSpec((None, d, 2 * MEM_W), lambda l, i: (l, 0, 0)),
                  pl.BlockSpec((None, 1, MEM_W), lambda l, i: (l, 0, 0)),
                  _const_spec(grp.shape)],
        out_specs=[spec_kv, spec_kv],
        out_shape=[jax.ShapeDtypeStruct((depth, b, m, MEM_W), BF16)] * 2,
        compiler_params=_params(("arbitrary", "arbitrary")),
    )(mem, g, w, kg, grp)


def _hg_tables():
    C = HG_CHUNK
    t = np.arange(C)[:, None]
    u = np.arange(C)[None, :]
    small, sgn = [], []
    for l in range(HG_LEVELS):
        c = C >> (l + 1)
        mid = (t // (2 * c)) * (2 * c) + c
        upper = t >= mid
        if l < HG_BIG_LEVELS:
            sgn.append(np.broadcast_to(np.where(upper, 1.0, -1.0), (C, C)))
        else:
            small.append(np.where(upper, (u >= mid) & (u <= t), (u > t) & (u < mid)))
    tri = (u <= t).astype(np.float32)
    small = np.concatenate(small, axis=0).astype(np.float32)
    sgn = np.concatenate(sgn, axis=0).astype(np.float32)
    p = np.floor(np.log2(np.maximum(t ^ u, 1))).astype(np.int32)
    lev = np.where(t > u, HG_LEVELS - 1 - p, np.where(t == u, HG_LEVELS, -1)).astype(np.int32)
    return tri, small, sgn, lev


def _hgrn_kernel(qs_ref, lf_ref, kk_ref, v_ref, sg_ref, tri_ref, small_ref, sgn_ref, lev_ref, og_ref,
                 y_ref, st_ref):
    C = HG_CHUNK
    heads = range(HG_HEADS)
    cols = [slice(h * HG_KDIM, (h + 1) * HG_KDIM) for h in heads]

    @pl.when(pl.program_id(1) == 0)
    def _():
        st_ref[...] = jnp.zeros_like(st_ref)

    hi, mid, lo = _split3(lf_ref[...])
    tri = tri_ref[...]
    small = small_ref[...]
    b = _dot(tri, hi) + _dot(tri, mid) + _dot(tri, lo)
    d_small = _dot(small, hi) + _dot(small, mid)
    lev = lev_ref[...]
    masks = [lev == l for l in range(HG_LEVELS + 1)]

    pairs, inter = [], []
    for h in heads:
        q = qs_ref[:, cols[h]]
        k = kk_ref[:, cols[h]]
        bh = b[:, cols[h]]
        ops = []
        for l in range(HG_LEVELS):
            if l < HG_BIG_LEVELS:
                c = C >> (l + 1)
                ref = jnp.concatenate([jnp.broadcast_to(bh[i:i + 1, :], (2 * c, HG_KDIM))
                                       for i in range(c - 1, C, 2 * c)], axis=0)
                e = jnp.exp2((bh - ref) * sgn_ref[l * C:(l + 1) * C, :])
            else:
                e = jnp.exp2(d_small[(l - HG_BIG_LEVELS) * C:(l - HG_BIG_LEVELS + 1) * C, cols[h]])
            ops.append(((q * e).astype(BF16), (k * e).astype(BF16)))
        ops.append((q.astype(BF16), k.astype(BF16)))
        pairs.append(ops)
        e_pre = jnp.exp2(bh)
        e_suf = jnp.exp2(bh[C - 1:C, :] - bh)
        inter.append(((q * e_pre).astype(BF16), (k * e_suf).astype(BF16), e_pre[C - 1:C, :]))

    scores = []
    for h in heads:
        a = jnp.zeros((C, C), F32)
        for l in range(HG_LEVELS + 1):
            a = jnp.where(masks[l], _dot_nt(*pairs[h][l]), a)
        scores.append(a.astype(BF16))

    for h in heads:
        qe, ke, dec = inter[h]
        v = v_ref[:, cols[h]]
        st = st_ref[h]
        o = _dot_nt(qe, st.astype(BF16)) + _dot(scores[h], v)
        st_ref[h] = st * dec + _dot_tn(v, ke)
        y_ref[:, cols[h]] = (_rms(o, og_ref[...]) * sg_ref[:, cols[h]]).astype(BF16)


def _hgrn(qs, lf, kk, v, sg, tables, og, batch, seq):
    n = qs.shape[0]
    nc = seq // HG_CHUNK
    row = lambda b, c: (b * nc + c, 0)
    spec = pl.BlockSpec((HG_CHUNK, HG_F), row)
    return pl.pallas_call(
        _hgrn_kernel,
        grid=(batch, nc),
        in_specs=[spec, spec, spec, spec, spec] + [_const_spec(t.shape) for t in tables]
                 + [_const_spec(og.shape)],
        out_specs=spec,
        out_shape=jax.ShapeDtypeStruct((n, HG_F), BF16),
        scratch_shapes=[pltpu.VMEM((HG_HEADS, HG_VDIM, HG_KDIM), F32)],
        compiler_params=_params(("arbitrary", "arbitrary")),
    )(qs, lf, kk, v, sg, *tables, og)


def _outproj_kernel(x_ref, y_ref, qm_ref, km_ref, vm_ref, qg_ref, grp_ref, w_ref, o_ref):
    q = qm_ref[...]
    hi, lo = _split2(q * q)
    ms = _dot(hi, grp_ref[...]) + _dot(lo, grp_ref[...])
    qn = q * lax.rsqrt(ms + EPS) * (qg_ref[...] * (MEM_HDIM ** -0.5))
    km = km_ref[...]
    vm = vm_ref[...]
    lane = lax.broadcasted_iota(jnp.int32, qn.shape, 1)
    m = jnp.zeros(qn.shape, F32)
    for h in range(MEM_HEADS):
        in_head = (lane >= h * MEM_HDIM) & (lane < (h + 1) * MEM_HDIM)
        s = _dot_nt(jnp.where(in_head, qn, 0.0).astype(BF16), km)
        p = jnp.exp(s - jnp.max(s, axis=-1, keepdims=True))
        o = _dot(p.astype(BF16), vm) / jnp.sum(p, axis=-1, keepdims=True)
        m = jnp.where(in_head, o, m)
    o_ref[...] = (x_ref[...] + _dot(y_ref[...], w_ref[0:MIX_W, :])
                  + _dot(m.astype(BF16), w_ref[MIX_W:MIX_W + MEM_W, :]))


def _outproj(x2, y, qm, kmem, vmem, layer, qg, grp, w, batch, seq, tm):
    n, d = x2.shape
    nt = seq // tm
    row = lambda b, t: (b * nt + t, 0)
    mem_spec = pl.BlockSpec((None, None) + kmem.shape[2:], lambda b, t: (layer, b, 0, 0))
    return pl.pallas_call(
        _outproj_kernel,
        grid=(batch, nt),
        in_specs=[pl.BlockSpec((tm, d), row), pl.BlockSpec((tm, MIX_W), row),
                  pl.BlockSpec((tm, MEM_W), row), mem_spec, mem_spec,
                  _const_spec(qg.shape), _const_spec(grp.shape), _const_spec(w.shape)],
        out_specs=pl.BlockSpec((tm, d), row),
        out_shape=jax.ShapeDtypeStruct((n, d), F32),
        compiler_params=_params(("arbitrary", "arbitrary")),
    )(x2, y, qm, kmem, vmem, qg, grp, w)


def _shift_rows(u, prev, d):
    r = pltpu.roll(u, d, axis=0)
    pr = pltpu.roll(prev, d, axis=0)
    sub = lax.broadcasted_iota(jnp.int32, pr.shape, 0)
    head = jnp.where(sub < d, pr, r[0:SUBLANES, :])
    return jnp.concatenate([head, r[SUBLANES:, :]], axis=0)


def _ffn_kernel(x_ref, g_ref, wup_ref, cw_ref, cb_ref, wdn_ref, o_ref, carry_ref, act_ref):
    tm = x_ref.shape[0]

    @pl.when(pl.program_id(1) == 0)
    def _():
        carry_ref[...] = jnp.zeros_like(carry_ref)

    x = x_ref[...]
    h = _rms(x, g_ref[...]).astype(BF16)

    def conv_slice(c0):
        cols = slice(c0, c0 + FFN_CHUNK)
        u = _dot(h, wup_ref[:, cols])
        prev = carry_ref[:, cols]
        carry_ref[:, cols] = u[tm - SUBLANES:, :]
        return (cw_ref[2:3, cols] * u + cw_ref[1:2, cols] * _shift_rows(u, prev, 1)
                + cw_ref[0:1, cols] * _shift_rows(u, prev, 2) + cb_ref[:, cols])

    for c in range(D_FF // FFN_CHUNK):
        a = conv_slice(c * FFN_CHUNK)
        b = conv_slice(D_FF + c * FFN_CHUNK)
        act_ref[:, c * FFN_CHUNK:(c + 1) * FFN_CHUNK] = (a * _sigmoid(a) * b).astype(BF16)
    o_ref[...] = x + _dot(act_ref[...], wdn_ref[...])


def _ffn(x2, g, wup, cw, cb, wdn, batch, seq, tm):
    n, d = x2.shape
    nt = seq // tm
    row = lambda b, t: (b * nt + t, 0)
    return pl.pallas_call(
        _ffn_kernel,
        grid=(batch, nt),
        in_specs=[pl.BlockSpec((tm, d), row), _const_spec(g.shape), _const_spec(wup.shape),
                  _const_spec(cw.shape), _const_spec(cb.shape), _const_spec(wdn.shape)],
        out_specs=pl.BlockSpec((tm, d), row),
        out_shape=jax.ShapeDtypeStruct((n, d), F32),
        scratch_shapes=[pltpu.VMEM((SUBLANES, 2 * D_FF), F32), pltpu.VMEM((tm, D_FF), BF16)],
        compiler_params=_params(("arbitrary", "arbitrary")),
    )(x2, g, wup, cw, cb, wdn)


def _mla_prep_kernel(x_ref, pos_ref, g_ref, win_ref, inv_ref, qag_ref, kvag_ref, wuq_ref, wukv_ref,
                     qgn_ref, qgc_ref, qgs_ref, kgn_ref, kgr_ref, sgn_ref,
                     q_ref, k_ref, v_ref, qm_ref):
    h = _rms(x_ref[...], g_ref[...]).astype(BF16)
    z = _dot(h, win_ref[...])
    cq = z[:, 0:MLA_Q_LORA]
    ckv = z[:, MLA_Q_LORA:MLA_Q_LORA + MLA_KV_LORA]
    kpe2 = z[:, MLA_Q_LORA + MLA_KV_LORA:MLA_Q_LORA + MLA_KV_LORA + LANES]
    qm_ref[...] = z[:, MLA_Q_LORA + MLA_KV_LORA + LANES:]

    ang = pos_ref[...].astype(F32) * inv_ref[...]
    cos = jnp.cos(ang)
    sin = jnp.sin(ang)

    qf = _dot(_rms(cq, qag_ref[...]).astype(BF16), wuq_ref[...])
    tq = cos * qgc_ref[...] + sin * qgs_ref[...]
    for hd in range(MLA_HEADS):
        qn = qf[:, hd * MLA_PAD:hd * MLA_PAD + MLA_NOPE]
        qr = qf[:, hd * MLA_PAD + MLA_NOPE:(hd + 1) * MLA_PAD]
        ss = jnp.sum(qn * qn, axis=-1, keepdims=True) + 0.5 * jnp.sum(qr * qr, axis=-1, keepdims=True)
        rs = lax.rsqrt(ss * (1.0 / MLA_QK) + EPS) * (MLA_QK ** -0.5 * LOG2E)
        q_ref[:, hd * MLA_PAD:hd * MLA_PAD + MLA_NOPE] = (qn * rs * qgn_ref[...]).astype(BF16)
        q_ref[:, hd * MLA_PAD + MLA_NOPE:(hd + 1) * MLA_PAD] = (qr * rs * tq).astype(BF16)

    kvf = _dot(_rms(ckv, kvag_ref[...]).astype(BF16), wukv_ref[...])
    pk = kpe2 * kgr_ref[...]
    kr = pk * cos + pltpu.roll(pk, MLA_ROPE // 2, axis=1) * (sin * sgn_ref[...])
    sk = 0.5 * jnp.sum(kpe2 * kpe2, axis=-1, keepdims=True)
    for hd in range(MLA_HEADS):
        kn = kvf[:, hd * MLA_PAD:hd * MLA_PAD + MLA_NOPE]
        ss = jnp.sum(kn * kn, axis=-1, keepdims=True) + sk
        rs = lax.rsqrt(ss * (1.0 / MLA_QK) + EPS)
        k_ref[:, hd * MLA_PAD:hd * MLA_PAD + MLA_NOPE] = (kn * rs * kgn_ref[...]).astype(BF16)
        k_ref[:, hd * MLA_PAD + MLA_NOPE:(hd + 1) * MLA_PAD] = (kr * rs).astype(BF16)
        v_ref[:, hd * MLA_VDIM:(hd + 1) * MLA_VDIM] = kvf[:, hd * MLA_PAD + MLA_NOPE:(hd + 1) * MLA_PAD].astype(BF16)


def _mla_prep(x2, pos, g, win, inv, qag, kvag, wuq, wukv, tabs, tm):
    n, d = x2.shape
    row = lambda i: (i, 0)
    consts = [g, win, inv, qag, kvag, wuq, wukv] + list(tabs)
    return pl.pallas_call(
        _mla_prep_kernel,
        grid=(n // tm,),
        in_specs=[pl.BlockSpec((tm, d), row), pl.BlockSpec((tm, 1), row)]
                 + [_const_spec(c.shape) for c in consts],
        out_specs=[pl.BlockSpec((tm, MLA_HEADS * MLA_PAD), row), pl.BlockSpec((tm, MLA_HEADS * MLA_PAD), row),
                   pl.BlockSpec((tm, MLA_HEADS * MLA_VDIM), row), pl.BlockSpec((tm, MEM_W), row)],
        out_shape=[jax.ShapeDtypeStruct((n, MLA_HEADS * MLA_PAD), BF16),
                   jax.ShapeDtypeStruct((n, MLA_HEADS * MLA_PAD), BF16),
                   jax.ShapeDtypeStruct((n, MLA_HEADS * MLA_VDIM), BF16),
                   jax.ShapeDtypeStruct((n, MEM_W), F32)],
        compiler_params=_params(("arbitrary",)),
    )(x2, pos, *consts)


def _flash_kernel(q_ref, k_ref, v_ref, o_ref):
    tq = q_ref.shape[0]
    n_tiles = k_ref.shape[0] // tq
    r = lax.broadcasted_iota(jnp.int32, (tq, tq), 0)
    c = lax.broadcasted_iota(jnp.int32, (tq, tq), 1)

    def attend(qi):
        past = qi * tq
        heads = range(FLASH_HEADS)
        kc = [slice(h * MLA_PAD, (h + 1) * MLA_PAD) for h in heads]
        vc = [slice(h * MLA_VDIM, (h + 1) * MLA_VDIM) for h in heads]
        sd = [_dot_nt(q_ref[:, kc[h]], k_ref[past:past + tq, kc[h]]) for h in heads]
        sp = [_dot_nt(q_ref[:, kc[h]], k_ref[0:past, kc[h]]) for h in heads] if qi > 0 else None
        pd, pp, ls = [], [], []
        for h in heads:
            s = jnp.where(c <= r, sd[h], -jnp.inf)
            m = jnp.max(s, axis=-1, keepdims=True)
            if qi > 0:
                m = jnp.maximum(m, jnp.max(sp[h], axis=-1, keepdims=True))
            p = jnp.exp2(s - m)
            l = jnp.sum(p, axis=-1, keepdims=True)
            pd.append(p.astype(BF16))
            if qi > 0:
                p = jnp.exp2(sp[h] - m)
                l = l + jnp.sum(p, axis=-1, keepdims=True)
                pp.append(p.astype(BF16))
            ls.append(l)
        for h in heads:
            acc = _dot(pd[h], v_ref[past:past + tq, vc[h]])
            if qi > 0:
                acc = acc + _dot(pp[h], v_ref[0:past, vc[h]])
            o_ref[:, vc[h]] = (acc / ls[h]).astype(BF16)

    for qi in range(n_tiles):
        pl.when(pl.program_id(2) == qi)(functools.partial(attend, qi))


def _flash(q, k, v, batch, seq, tq):
    q3 = q.reshape(batch, seq, MLA_HEADS * MLA_PAD)
    k3 = k.reshape(batch, seq, MLA_HEADS * MLA_PAD)
    v3 = v.reshape(batch, seq, MLA_HEADS * MLA_VDIM)
    wq = FLASH_HEADS * MLA_PAD
    wv = FLASH_HEADS * MLA_VDIM
    out = pl.pallas_call(
        _flash_kernel,
        grid=(batch, MLA_HEADS // FLASH_HEADS, seq // tq),
        in_specs=[pl.BlockSpec((None, tq, wq), lambda b, h, i: (b, i, h)),
                  pl.BlockSpec((None, seq, wq), lambda b, h, i: (b, 0, h)),
                  pl.BlockSpec((None, seq, wv), lambda b, h, i: (b, 0, h))],
        out_specs=pl.BlockSpec((None, tq, wv), lambda b, h, i: (b, i, h)),
        out_shape=jax.ShapeDtypeStruct((batch, seq, MLA_HEADS * MLA_VDIM), BF16),
        compiler_params=_params(("arbitrary", "arbitrary", "arbitrary")),
    )(q3, k3, v3)
    return out.reshape(batch * seq, MLA_HEADS * MLA_VDIM)


def _mla_layouts(w_in, w_uq, q_norm_g, k_norm_g):
    half = MLA_ROPE // 2
    c0 = MLA_Q_LORA + MLA_KV_LORA
    kpe = w_in[:, c0:c0 + MLA_ROPE]
    win = jnp.concatenate([w_in[:, :c0], kpe, kpe, w_in[:, c0 + MLA_ROPE:]], axis=1)
    wq = w_uq.reshape(MLA_Q_LORA, MLA_HEADS, MLA_QK)
    x1 = wq[:, :, MLA_NOPE:MLA_NOPE + half]
    x2 = wq[:, :, MLA_NOPE + half:]
    wuq = jnp.concatenate([wq[:, :, :MLA_NOPE], x1, x2, x2, x1], axis=2).reshape(MLA_Q_LORA, MLA_HEADS * MLA_PAD)
    z = jnp.zeros((half,), F32)
    g1, g2 = q_norm_g[MLA_NOPE:MLA_NOPE + half], q_norm_g[MLA_NOPE + half:]
    qgn = q_norm_g[:MLA_NOPE][None, :]
    qgc = jnp.concatenate([g1, g2, z, z])[None, :]
    qgs = jnp.concatenate([z, z, -g2, g1])[None, :]
    kgn = k_norm_g[:MLA_NOPE][None, :]
    kgr = jnp.concatenate([k_norm_g[MLA_NOPE:], k_norm_g[MLA_NOPE:]])[None, :]
    one = jnp.ones((half,), F32)
    sgn = jnp.concatenate([-one, one, -one, one])[None, :]
    return win, wuq, (qgn, qgc, qgs, kgn, kgr, sgn)


def kernel(x, mem, positions, mix_norm_g, ffn_norm_g, mem_norm_g, w_mem_kv, mem_q_norm_g, mem_k_norm_g, w_out, w_up, conv_w, conv_b, w_down, hg_w_in, hg_lb_logits, hg_out_norm_g, mla_w_in, mla_qa_norm_g, mla_kva_norm_g, mla_w_uq, mla_w_ukv, mla_q_norm_g, mla_k_norm_g):
    batch, seq, d = x.shape
    n = batch * seq
    tm = min(256, seq)
    depth = mix_norm_g.shape[0]
    x2 = x.reshape(n, d)

    grp = jnp.asarray(np.kron(np.eye(MEM_HEADS), np.full((MEM_HDIM, MEM_HDIM), 1.0 / MEM_HDIM)), BF16)
    tri, small, sgn, lev = _hg_tables()
    hg_tables = (jnp.asarray(tri, BF16), jnp.asarray(small, BF16), jnp.asarray(sgn), jnp.asarray(lev))

    kmem, vmem = _memkv(mem, mem_norm_g[:, None, :], w_mem_kv.astype(BF16),
                        jnp.tile(mem_k_norm_g, (1, MEM_HEADS))[:, None, :], grp)
    qg = jnp.tile(mem_q_norm_g, (1, MEM_HEADS))

    def finish_layer(xin, y, qm, layer):
        x1 = _outproj(xin, y, qm, kmem, vmem, layer, qg[layer][None, :], grp,
                      w_out[layer].astype(BF16), batch, seq, tm)
        return _ffn(x1, ffn_norm_g[layer][None, :], w_up[layer].astype(BF16), conv_w[layer],
                    conv_b[layer][None, :], w_down[layer].astype(BF16), batch, seq, tm)

    qs, lf, kk, v, sg, qm = _hg_inproj(x2, mix_norm_g[0][None, :], hg_w_in[0].astype(BF16),
                                        hg_lb_logits, tm)
    y = _hgrn(qs, lf, kk, v, sg, hg_tables, hg_out_norm_g[0][None, :], batch, seq)
    x2 = finish_layer(x2, y, qm, 0)

    win, wuq, tabs = _mla_layouts(mla_w_in[0], mla_w_uq[0], mla_q_norm_g[0], mla_k_norm_g[0])
    half = MLA_ROPE // 2
    inv = ROPE_THETA ** (-jnp.arange(half, dtype=F32) / half)
    inv = jnp.tile(inv, LANES // half)[None, :]
    q, k, v, qm = _mla_prep(x2, positions.reshape(n, 1), mix_norm_g[1][None, :], win.astype(BF16), inv,
                            mla_qa_norm_g[0][None, :], mla_kva_norm_g[0][None, :], wuq.astype(BF16),
                            mla_w_ukv[0].astype(BF16), tabs, tm)
    y = _flash(q, k, v, batch, seq, min(FLASH_TQ, seq))
    x2 = finish_layer(x2, y, qm, 1)
    return x2.reshape(batch, seq, d)
```

```python
import functools

import numpy as np
import jax
import jax.numpy as jnp
from jax import lax
from jax.experimental import pallas as pl
from jax.experimental.pallas import tpu as pltpu

F32 = jnp.float32
BF16 = jnp.bfloat16
EPS = 1e-6

HG_HEADS = 6
HG_KDIM = 128
HG_VDIM = 128
HG_F = HG_HEADS * HG_KDIM
MLA_HEADS = 6
MLA_Q_LORA = 384
MLA_KV_LORA = 256
MLA_NOPE = 128
MLA_ROPE = 64
MLA_QK = MLA_NOPE + MLA_ROPE
MLA_VDIM = 128
MLA_PAD = 256
ROPE_THETA = 10000.0
MEM_HEADS = 4
MEM_HDIM = 64
MEM_W = MEM_HEADS * MEM_HDIM
MIX_W = 768
D_FF = 2816
CONV_W = 3

LANES = 128
SUBLANES = 8
VMEM_LIMIT = 56 * 1024 * 1024

HG_CHUNK = 128
HG_LEVELS = 7
HG_BIG_LEVELS = 4
FFN_CHUNK = 256
NEG_BIG = -1e30
LOG2E = 1.4426950408889634
FLASH_HEADS = 2
FLASH_TQ = 512
ROPE_ROWS = 512
MLA_SUB = 256


def _dot(a, b):
    return jnp.dot(a, b, preferred_element_type=F32)


def _dot_nt(a, b):
    return lax.dot_general(a, b, (((1,), (1,)), ((), ())), preferred_element_type=F32)


def _dot_tn(a, b):
    return lax.dot_general(a, b, (((0,), (0,)), ((), ())), preferred_element_type=F32)


def _rms(x, g):
    ms = jnp.mean(x * x, axis=-1, keepdims=True)
    return x * lax.rsqrt(ms + EPS) * g


def _sigmoid(x):
    return 1.0 / (1.0 + jnp.exp(-x))


def _split2(x):
    hi = x.astype(BF16)
    lo = (x - hi.astype(F32)).astype(BF16)
    return hi, lo


def _split3(x):
    hi = x.astype(BF16)
    r = x - hi.astype(F32)
    mid = r.astype(BF16)
    lo = (r - mid.astype(F32)).astype(BF16)
    return hi, mid, lo


def _const_spec(shape):
    nd = len(shape)
    return pl.BlockSpec(shape, lambda *_: (0,) * nd, pipeline_mode=pl.Buffered(1))


def _params(sem):
    return pltpu.CompilerParams(dimension_semantics=sem, vmem_limit_bytes=VMEM_LIMIT)


def _hg_inproj_kernel(x_ref, g_ref, w_ref, lbl_ref, qs_ref, lf_ref, kk_ref, v_ref, sg_ref, qm_ref):
    h = _rms(x_ref[...], g_ref[...]).astype(BF16)
    l = lbl_ref[...]
    e = jnp.exp(l - jnp.max(l, axis=0, keepdims=True))
    lb = e[0:1, :] / jnp.sum(e, axis=0, keepdims=True)
    q = _dot(h, w_ref[:, 0:HG_F])
    qs_ref[...] = q * _sigmoid(q) * (HG_KDIM ** -0.5)
    f = _dot(h, w_ref[:, HG_F:2 * HG_F])
    fg = lb + (1.0 - lb) * _sigmoid(f)
    lf_ref[...] = jnp.log2(fg)
    kk_ref[...] = 1.0 - fg
    v_ref[...] = _dot(h, w_ref[:, 2 * HG_F:3 * HG_F]).astype(BF16)
    gate = _dot(h, w_ref[:, 3 * HG_F:4 * HG_F])
    sg_ref[...] = gate * _sigmoid(gate)
    qm_ref[...] = _dot(h, w_ref[:, 4 * HG_F:4 * HG_F + MEM_W])


def _hg_inproj(x2, g, w, lb_logits, tm):
    n, d = x2.shape
    row = lambda i: (i, 0)
    out768 = pl.BlockSpec((tm, HG_F), row)
    return pl.pallas_call(
        _hg_inproj_kernel,
        grid=(n // tm,),
        in_specs=[pl.BlockSpec((tm, d), row), _const_spec((1, d)), _const_spec(w.shape),
                  _const_spec(lb_logits.shape)],
        out_specs=[out768, out768, out768, out768, out768, pl.BlockSpec((tm, MEM_W), row)],
        out_shape=[jax.ShapeDtypeStruct((n, HG_F), F32), jax.ShapeDtypeStruct((n, HG_F), F32),
                   jax.ShapeDtypeStruct((n, HG_F), F32), jax.ShapeDtypeStruct((n, HG_F), BF16),
                   jax.ShapeDtypeStruct((n, HG_F), F32), jax.ShapeDtypeStruct((n, MEM_W), F32)],
        compiler_params=_params(("arbitrary",)),
    )(x2, g, w, lb_logits)


def _memkv_kernel(mem_ref, g_ref, w_ref, kg_ref, grp_ref, k_ref, v_ref):
    mn = _rms(mem_ref[...], g_ref[...]).astype(BF16)
    kv = _dot(mn, w_ref[...])
    k = kv[:, :MEM_W]
    hi, lo = _split2(k * k)
    ms = _dot(hi, grp_ref[...]) + _dot(lo, grp_ref[...])
    k_ref[...] = (k * lax.rsqrt(ms + EPS) * kg_ref[...]).astype(BF16)
    v_ref[...] = kv[:, MEM_W:].astype(BF16)


def _memkv(mem, g, w, kg, grp):
    depth = g.shape[0]
    b, m, d = mem.shape
    spec_kv = pl.BlockSpec((None, None, m, MEM_W), lambda l, i: (l, i, 0, 0))
    return pl.pallas_call(
        _memkv_kernel,
        grid=(depth, b),
        in_specs=[pl.BlockSpec((None, m, d), lambda l, i: (i, 0, 0)),
                  pl.BlockSpec((None, 1, d), lambda l, i: (l, 0, 0)),
                  pl.BlockSpec((None, d, 2 * MEM_W), lambda l, i: (l, 0, 0)),
                  pl.BlockSpec((None, 1, MEM_W), lambda l, i: (l, 0, 0)),
                  _const_spec(grp.shape)],
        out_specs=[spec_kv, spec_kv],
        out_shape=[jax.ShapeDtypeStruct((depth, b, m, MEM_W), BF16)] * 2,
        compiler_params=_params(("arbitrary", "arbitrary")),
    )(mem, g, w, kg, grp)


def _hg_tables():
    C = HG_CHUNK
    t = np.arange(C)[:, None]
    u = np.arange(C)[None, :]
    small, sgn = [], []
    for l in range(HG_LEVELS):
        c = C >> (l + 1)
        mid = (t // (2 * c)) * (2 * c) + c
        upper = t >= mid
        if l < HG_BIG_LEVELS:
            sgn.append(np.broadcast_to(np.where(upper, 1.0, -1.0), (C, C)))
        else:
            small.append(np.where(upper, (u >= mid) & (u <= t), (u > t) & (u < mid)))
    tri = (u <= t).astype(np.float32)
    sums = np.concatenate([tri] + small, axis=0).astype(np.float32)
    sums = np.concatenate([sums, sums], axis=1)
    sgn = np.concatenate(sgn, axis=0).astype(np.float32)
    p = np.floor(np.log2(np.maximum(t ^ u, 1))).astype(np.int32)
    lev = np.where(t > u, HG_LEVELS - 1 - p, np.where(t == u, HG_LEVELS, -1)).astype(np.int32)
    return sums, tri, sgn, lev


def _hgrn_kernel(qs_ref, lf_ref, kk_ref, v_ref, sg_ref, sums_ref, tri_ref, sgn_ref, lev_ref, og_ref,
                 y_ref, st_ref):
    C = HG_CHUNK
    heads = range(HG_HEADS)
    cols = [slice(h * HG_KDIM, (h + 1) * HG_KDIM) for h in heads]

    @pl.when(pl.program_id(1) == 0)
    def _():
        st_ref[...] = jnp.zeros_like(st_ref)

    hi, mid, lo = _split3(lf_ref[...])
    d_all = _dot(sums_ref[...], jnp.concatenate([hi, mid], axis=0))
    b = d_all[0:C, :] + _dot(tri_ref[...], lo)
    d_small = d_all[C:, :]
    lev = lev_ref[...]
    masks = [lev == l for l in range(HG_LEVELS + 1)]

    pairs, inter, diag = [], [], []
    for h in heads:
        q = qs_ref[:, cols[h]]
        k = kk_ref[:, cols[h]]
        diag.append(jnp.sum(q * k, axis=-1, keepdims=True))
        qb = q.astype(BF16)
        kb = k.astype(BF16)
        bh = b[:, cols[h]]
        ops = []
        for l in range(HG_LEVELS):
            if l < HG_BIG_LEVELS:
                c = C >> (l + 1)
                ref = jnp.concatenate([jnp.broadcast_to(bh[i:i + 1, :], (2 * c, HG_KDIM))
                                       for i in range(c - 1, C, 2 * c)], axis=0)
                e = jnp.exp2((bh - ref) * sgn_ref[l * C:(l + 1) * C, :])
            else:
                e = jnp.exp2(d_small[(l - HG_BIG_LEVELS) * C:(l - HG_BIG_LEVELS + 1) * C, cols[h]])
            eb = e.astype(BF16)
            ops.append((qb * eb, kb * eb))
        pairs.append(ops)
        e_pre = jnp.exp2(bh)
        e_suf = jnp.exp2(bh[C - 1:C, :] - bh)
        inter.append((qb * e_pre.astype(BF16), kb * e_suf.astype(BF16), e_pre[C - 1:C, :]))

    scores = []
    for h in heads:
        a = jnp.where(masks[HG_LEVELS], diag[h], 0.0)
        for l in range(HG_LEVELS):
            a = jnp.where(masks[l], _dot_nt(*pairs[h][l]), a)
        scores.append(a.astype(BF16))

    for h in heads:
        qe, ke, dec = inter[h]
        v = v_ref[:, cols[h]]
        st = st_ref[h]
        o = _dot_nt(qe, st.astype(BF16)) + _dot(scores[h], v)
        st_ref[h] = st * dec + _dot_tn(v, ke)
        y_ref[:, cols[h]] = (_rms(o, og_ref[...]) * sg_ref[:, cols[h]]).astype(BF16)


def _hgrn(qs, lf, kk, v, sg, tables, og, batch, seq):
    n = qs.shape[0]
    nc = seq // HG_CHUNK
    row = lambda b, c: (b * nc + c, 0)
    spec = pl.BlockSpec((HG_CHUNK, HG_F), row)
    return pl.pallas_call(
        _hgrn_kernel,
        grid=(batch, nc),
        in_specs=[spec, spec, spec, spec, spec] + [_const_spec(t.shape) for t in tables]
                 + [_const_spec(og.shape)],
        out_specs=spec,
        out_shape=jax.ShapeDtypeStruct((n, HG_F), BF16),
        scratch_shapes=[pltpu.VMEM((HG_HEADS, HG_VDIM, HG_KDIM), F32)],
        compiler_params=_params(("arbitrary", "arbitrary")),
    )(qs, lf, kk, v, sg, *tables, og)


def _outproj_kernel(x_ref, y_ref, qm_ref, km_ref, vm_ref, qg_ref, grp_ref, w_ref, o_ref):
    q = qm_ref[...]
    hi, lo = _split2(q * q)
    ms = _dot(hi, grp_ref[...]) + _dot(lo, grp_ref[...])
    qn = (q * lax.rsqrt(ms + EPS) * (qg_ref[...] * (MEM_HDIM ** -0.5 * LOG2E))).astype(BF16)
    tm = qn.shape[0]
    head = lax.broadcasted_iota(jnp.int32, qn.shape, 1) // MEM_HDIM
    qs = jnp.concatenate([jnp.where(head == h, qn, jnp.zeros_like(qn)) for h in range(MEM_HEADS)], axis=0)
    s = _dot_nt(qs, km_ref[...])
    p = jnp.exp2(s - jnp.max(s, axis=-1, keepdims=True))
    o = _dot(p.astype(BF16), vm_ref[...]) / jnp.sum(p, axis=-1, keepdims=True)
    m = o[0:tm, :]
    for h in range(1, MEM_HEADS):
        m = jnp.where(head == h, o[h * tm:(h + 1) * tm, :], m)
    o_ref[...] = (x_ref[...] + _dot(y_ref[...], w_ref[0:MIX_W, :])
                  + _dot(m.astype(BF16), w_ref[MIX_W:MIX_W + MEM_W, :]))


def _outproj(x2, y, qm, kmem, vmem, layer, qg, grp, w, batch, seq, tm):
    n, d = x2.shape
    nt = seq // tm
    row = lambda b, t: (b * nt + t, 0)
    mem_spec = pl.BlockSpec((None, None) + kmem.shape[2:], lambda b, t: (layer, b, 0, 0))
    return pl.pallas_call(
        _outproj_kernel,
        grid=(batch, nt),
        in_specs=[pl.BlockSpec((tm, d), row), pl.BlockSpec((tm, MIX_W), row),
                  pl.BlockSpec((tm, MEM_W), row), mem_spec, mem_spec,
                  _const_spec(qg.shape), _const_spec(grp.shape), _const_spec(w.shape)],
        out_specs=pl.BlockSpec((tm, d), row),
        out_shape=jax.ShapeDtypeStruct((n, d), F32),
        compiler_params=_params(("arbitrary", "arbitrary")),
    )(x2, y, qm, kmem, vmem, qg, grp, w)


def _shift_row(t, prev):
    r = pltpu.roll(t, 1, axis=0)
    pr = pltpu.roll(prev, 1, axis=0)
    sub = lax.broadcasted_iota(jnp.int32, pr.shape, 0)
    head = jnp.where(sub < 1, pr, r[0:SUBLANES, :])
    return jnp.concatenate([head, r[SUBLANES:, :]], axis=0)


def _ffn_kernel(x_ref, g_ref, wup_ref, cw_ref, cb_ref, wdn_ref, o_ref, carry_ref, act_ref):
    tm = x_ref.shape[0]

    @pl.when(pl.program_id(1) == 0)
    def _():
        carry_ref[...] = jnp.zeros_like(carry_ref)

    x = x_ref[...]
    h = _rms(x, g_ref[...]).astype(BF16)

    def conv_slice(c0):
        cols = slice(c0, c0 + FFN_CHUNK)
        u = _dot(h, wup_ref[:, cols])
        prev = carry_ref[:, cols]
        carry_ref[:, cols] = u[tm - SUBLANES:, :]
        w0, w1, w2 = cw_ref[0:1, cols], cw_ref[1:2, cols], cw_ref[2:3, cols]
        t0, p0 = w0 * u, w0 * prev
        t1 = _shift_row(t0, p0) + w1 * u
        p1 = pltpu.roll(p0, 1, axis=0) + w1 * prev
        return _shift_row(t1, p1) + (w2 * u + cb_ref[:, cols])

    for c in range(D_FF // FFN_CHUNK):
        a = conv_slice(c * FFN_CHUNK).astype(BF16)
        b = conv_slice(D_FF + c * FFN_CHUNK).astype(BF16)
        sig = 1.0 / (1.0 + jnp.exp2(a * (-LOG2E)))
        act_ref[:, c * FFN_CHUNK:(c + 1) * FFN_CHUNK] = a * b * sig
    o_ref[...] = x + _dot(act_ref[...], wdn_ref[...])


def _ffn(x2, g, wup, cw, cb, wdn, batch, seq, tm):
    n, d = x2.shape
    nt = seq // tm
    row = lambda b, t: (b * nt + t, 0)
    return pl.pallas_call(
        _ffn_kernel,
        grid=(batch, nt),
        in_specs=[pl.BlockSpec((tm, d), row), _const_spec(g.shape), _const_spec(wup.shape),
                  _const_spec(cw.shape), _const_spec(cb.shape), _const_spec(wdn.shape)],
        out_specs=pl.BlockSpec((tm, d), row),
        out_shape=jax.ShapeDtypeStruct((n, d), F32),
        scratch_shapes=[pltpu.VMEM((SUBLANES, 2 * D_FF), F32), pltpu.VMEM((tm, D_FF), BF16)],
        compiler_params=_params(("arbitrary", "arbitrary")),
    )(x2, g, wup, cw, cb, wdn)


def _rope_kernel(pos_ref, inv_ref, cos_ref, sin_ref):
    ang = pos_ref[...].astype(F32) * inv_ref[...]
    cos_ref[...] = jnp.cos(ang)
    sin_ref[...] = jnp.sin(ang)


def _rope_tables(positions):
    half = MLA_ROPE // 2
    per_row = LANES // half
    n = positions.size
    rows = n // per_row
    tr = min(ROPE_ROWS, rows)
    inv = ROPE_THETA ** (-jnp.arange(half, dtype=F32) / half)
    pos = jnp.broadcast_to(positions.reshape(n, 1), (n, half)).reshape(rows, LANES)
    spec = pl.BlockSpec((tr, LANES), lambda i: (i, 0))
    cos, sin = pl.pallas_call(
        _rope_kernel,
        grid=(rows // tr,),
        in_specs=[spec, _const_spec((1, LANES))],
        out_specs=[spec, spec],
        out_shape=[jax.ShapeDtypeStruct((rows, LANES), F32)] * 2,
        compiler_params=_params(("arbitrary",)),
    )(pos, jnp.tile(inv, per_row)[None, :])
    return jnp.concatenate([cos.reshape(n, half), sin.reshape(n, half)], axis=1)


def _rope_expand():
    half = MLA_ROPE // 2
    e = np.zeros((2 * half, 3 * LANES), np.float32)
    j = np.arange(half)
    for blk in range(2):
        e[j, blk * half + j] = 1.0
        e[half + j, (2 + blk) * half + j] = 1.0
    for blk in range(4):
        e[j, LANES + blk * half + j] = 1.0
        e[half + j, 2 * LANES + blk * half + j] = -1.0 if blk % 2 == 0 else 1.0
    return e


def _mla_prep_kernel(x_ref, cs_ref, g_ref, win_ref, exp_ref, qag_ref, kvag_ref, wuq_ref, wukv_ref,
                     qgn_ref, qgr_ref, kgn_ref, kgr_ref,
                     q_ref, k_ref, v_ref, qm_ref):
    subs = [slice(r * MLA_SUB, (r + 1) * MLA_SUB) for r in range(x_ref.shape[0] // MLA_SUB)]
    c0 = MLA_Q_LORA + MLA_KV_LORA
    zs = [_dot(_rms(x_ref[rows, :], g_ref[...]).astype(BF16), win_ref[...]) for rows in subs]
    qfs = [_dot(_rms(z[:, 0:MLA_Q_LORA], qag_ref[...]).astype(BF16), wuq_ref[...]) for z in zs]
    kvfs = [_dot(_rms(z[:, MLA_Q_LORA:c0], kvag_ref[...]).astype(BF16), wukv_ref[...]) for z in zs]

    for rows, z, qf, kvf in zip(subs, zs, qfs, kvfs):
        kpe2 = z[:, c0:c0 + LANES]
        qm_ref[rows, :] = z[:, c0 + LANES:]

        hi, lo = _split2(cs_ref[rows, :])
        rope = _dot(hi, exp_ref[...]) + _dot(lo, exp_ref[...])
        tq = rope[:, 0:LANES] * qgr_ref[...]
        cos = rope[:, LANES:2 * LANES]
        sin_signed = rope[:, 2 * LANES:3 * LANES]

        for hd in range(MLA_HEADS):
            qn = qf[:, hd * MLA_PAD:hd * MLA_PAD + MLA_NOPE]
            qr = qf[:, hd * MLA_PAD + MLA_NOPE:(hd + 1) * MLA_PAD]
            ss = jnp.sum(qn * qn, axis=-1, keepdims=True) + 0.5 * jnp.sum(qr * qr, axis=-1, keepdims=True)
            rs = lax.rsqrt(ss * (1.0 / MLA_QK) + EPS) * (MLA_QK ** -0.5 * LOG2E)
            q_ref[rows, hd * MLA_PAD:hd * MLA_PAD + MLA_NOPE] = (qn * rs * qgn_ref[...]).astype(BF16)
            q_ref[rows, hd * MLA_PAD + MLA_NOPE:(hd + 1) * MLA_PAD] = (qr * rs * tq).astype(BF16)

        pk = kpe2 * kgr_ref[...]
        kr = pk * cos + pltpu.roll(pk, MLA_ROPE // 2, axis=1) * sin_signed
        sk = 0.5 * jnp.sum(kpe2 * kpe2, axis=-1, keepdims=True)
        for hd in range(MLA_HEADS):
            kn = kvf[:, hd * MLA_PAD:hd * MLA_PAD + MLA_NOPE]
            ss = jnp.sum(kn * kn, axis=-1, keepdims=True) + sk
            rs = lax.rsqrt(ss * (1.0 / MLA_QK) + EPS)
            k_ref[rows, hd * MLA_PAD:hd * MLA_PAD + MLA_NOPE] = (kn * rs * kgn_ref[...]).astype(BF16)
            k_ref[rows, hd * MLA_PAD + MLA_NOPE:(hd + 1) * MLA_PAD] = (kr * rs).astype(BF16)
            v_ref[rows, hd * MLA_VDIM:(hd + 1) * MLA_VDIM] = kvf[:, hd * MLA_PAD + MLA_NOPE:(hd + 1) * MLA_PAD].astype(BF16)


def _mla_prep(x2, cs, g, win, expand, qag, kvag, wuq, wukv, tabs, tm):
    n, d = x2.shape
    row = lambda i: (i, 0)
    consts = [g, win, expand, qag, kvag, wuq, wukv] + list(tabs)
    return pl.pallas_call(
        _mla_prep_kernel,
        grid=(n // tm,),
        in_specs=[pl.BlockSpec((tm, d), row), pl.BlockSpec((tm, MLA_ROPE), row)]
                 + [_const_spec(c.shape) for c in consts],
        out_specs=[pl.BlockSpec((tm, MLA_HEADS * MLA_PAD), row), pl.BlockSpec((tm, MLA_HEADS * MLA_PAD), row),
                   pl.BlockSpec((tm, MLA_HEADS * MLA_VDIM), row), pl.BlockSpec((tm, MEM_W), row)],
        out_shape=[jax.ShapeDtypeStruct((n, MLA_HEADS * MLA_PAD), BF16),
                   jax.ShapeDtypeStruct((n, MLA_HEADS * MLA_PAD), BF16),
                   jax.ShapeDtypeStruct((n, MLA_HEADS * MLA_VDIM), BF16),
                   jax.ShapeDtypeStruct((n, MEM_W), F32)],
        compiler_params=_params(("arbitrary",)),
    )(x2, cs, *consts)


def _flash_kernel(q_ref, k_ref, v_ref, o_ref):
    tq = q_ref.shape[0]
    n_tiles = k_ref.shape[0] // tq
    r = lax.broadcasted_iota(jnp.int32, (tq, tq), 0)
    c = lax.broadcasted_iota(jnp.int32, (tq, tq), 1)

    def attend(qi):
        past = qi * tq
        heads = range(FLASH_HEADS)
        kc = [slice(h * MLA_PAD, (h + 1) * MLA_PAD) for h in heads]
        vc = [slice(h * MLA_VDIM, (h + 1) * MLA_VDIM) for h in heads]
        sd = [_dot_nt(q_ref[:, kc[h]], k_ref[past:past + tq, kc[h]]) for h in heads]
        sp = [_dot_nt(q_ref[:, kc[h]], k_ref[0:past, kc[h]]) for h in heads] if qi > 0 else None
        pd, pp, ls = [], [], []
        for h in heads:
            s = jnp.where(c <= r, sd[h], -jnp.inf)
            m = jnp.max(s, axis=-1, keepdims=True)
            if qi > 0:
                m = jnp.maximum(m, jnp.max(sp[h], axis=-1, keepdims=True))
            p = jnp.exp2(s - m)
            l = jnp.sum(p, axis=-1, keepdims=True)
            pd.append(p.astype(BF16))
            if qi > 0:
                p = jnp.exp2(sp[h] - m)
                l = l + jnp.sum(p, axis=-1, keepdims=True)
                pp.append(p.astype(BF16))
            ls.append(l)
        for h in heads:
            acc = _dot(pd[h], v_ref[past:past + tq, vc[h]])
            if qi > 0:
                acc = acc + _dot(pp[h], v_ref[0:past, vc[h]])
            o_ref[:, vc[h]] = (acc / ls[h]).astype(BF16)

    for qi in range(n_tiles):
        pl.when(pl.program_id(2) == qi)(functools.partial(attend, qi))


def _flash(q, k, v, batch, seq, tq):
    q3 = q.reshape(batch, seq, MLA_HEADS * MLA_PAD)
    k3 = k.reshape(batch, seq, MLA_HEADS * MLA_PAD)
    v3 = v.reshape(batch, seq, MLA_HEADS * MLA_VDIM)
    wq = FLASH_HEADS * MLA_PAD
    wv = FLASH_HEADS * MLA_VDIM
    out = pl.pallas_call(
        _flash_kernel,
        grid=(batch, MLA_HEADS // FLASH_HEADS, seq // tq),
        in_specs=[pl.BlockSpec((None, tq, wq), lambda b, h, i: (b, i, h)),
                  pl.BlockSpec((None, seq, wq), lambda b, h, i: (b, 0, h)),
                  pl.BlockSpec((None, seq, wv), lambda b, h, i: (b, 0, h))],
        out_specs=pl.BlockSpec((None, tq, wv), lambda b, h, i: (b, i, h)),
        out_shape=jax.ShapeDtypeStruct((batch, seq, MLA_HEADS * MLA_VDIM), BF16),
        compiler_params=_params(("arbitrary", "arbitrary", "arbitrary")),
    )(q3, k3, v3)
    return out.reshape(batch * seq, MLA_HEADS * MLA_VDIM)


def _mla_layouts(w_in, w_uq, q_norm_g, k_norm_g):
    half = MLA_ROPE // 2
    c0 = MLA_Q_LORA + MLA_KV_LORA
    kpe = w_in[:, c0:c0 + MLA_ROPE]
    win = jnp.concatenate([w_in[:, :c0], kpe, kpe, w_in[:, c0 + MLA_ROPE:]], axis=1)
    wq = w_uq.reshape(MLA_Q_LORA, MLA_HEADS, MLA_QK)
    x1 = wq[:, :, MLA_NOPE:MLA_NOPE + half]
    x2 = wq[:, :, MLA_NOPE + half:]
    wuq = jnp.concatenate([wq[:, :, :MLA_NOPE], x1, x2, x2, x1], axis=2).reshape(MLA_Q_LORA, MLA_HEADS * MLA_PAD)
    g1, g2 = q_norm_g[MLA_NOPE:MLA_NOPE + half], q_norm_g[MLA_NOPE + half:]
    qgn = q_norm_g[:MLA_NOPE][None, :]
    qgr = jnp.concatenate([g1, g2, -g2, g1])[None, :]
    kgn = k_norm_g[:MLA_NOPE][None, :]
    kgr = jnp.concatenate([k_norm_g[MLA_NOPE:], k_norm_g[MLA_NOPE:]])[None, :]
    return win, wuq, (qgn, qgr, kgn, kgr)


def kernel(x, mem, positions, mix_norm_g, ffn_norm_g, mem_norm_g, w_mem_kv, mem_q_norm_g, mem_k_norm_g, w_out, w_up, conv_w, conv_b, w_down, hg_w_in, hg_lb_logits, hg_out_norm_g, mla_w_in, mla_qa_norm_g, mla_kva_norm_g, mla_w_uq, mla_w_ukv, mla_q_norm_g, mla_k_norm_g):
    batch, seq, d = x.shape
    n = batch * seq
    tm = min(256, seq)
    tm_big = min(512, seq)
    x2 = x.reshape(n, d)

    grp = jnp.asarray(np.kron(np.eye(MEM_HEADS), np.full((MEM_HDIM, MEM_HDIM), 1.0 / MEM_HDIM)), BF16)
    sums, tri, sgn, lev = _hg_tables()
    hg_tables = (jnp.asarray(sums, BF16), jnp.asarray(tri, BF16), jnp.asarray(sgn), jnp.asarray(lev))

    kmem, vmem = _memkv(mem, mem_norm_g[:, None, :], w_mem_kv.astype(BF16),
                        jnp.tile(mem_k_norm_g, (1, MEM_HEADS))[:, None, :], grp)
    qg = jnp.tile(mem_q_norm_g, (1, MEM_HEADS))

    def finish_layer(xin, y, qm, layer):
        x1 = _outproj(xin, y, qm, kmem, vmem, layer, qg[layer][None, :], grp,
                      w_out[layer].astype(BF16), batch, seq, tm_big)
        return _ffn(x1, ffn_norm_g[layer][None, :], w_up[layer].astype(BF16), conv_w[layer],
                    conv_b[layer][None, :], w_down[layer].astype(BF16), batch, seq, tm_big)

    qs, lf, kk, v, sg, qm = _hg_inproj(x2, mix_norm_g[0][None, :], hg_w_in[0].astype(BF16),
                                        hg_lb_logits, tm)
    y = _hgrn(qs, lf, kk, v, sg, hg_tables, hg_out_norm_g[0][None, :], batch, seq)
    x2 = finish_layer(x2, y, qm, 0)

    win, wuq, tabs = _mla_layouts(mla_w_in[0], mla_w_uq[0], mla_q_norm_g[0], mla_k_norm_g[0])
    q, k, v, qm = _mla_prep(x2, _rope_tables(positions), mix_norm_g[1][None, :], win.astype(BF16),
                            jnp.asarray(_rope_expand(), BF16),
                            mla_qa_norm_g[0][None, :], mla_kva_norm_g[0][None, :], wuq.astype(BF16),
                            mla_w_ukv[0].astype(BF16), tabs, tm_big)
    y = _flash(q, k, v, batch, seq, min(FLASH_TQ, seq))
    x2 = finish_layer(x2, y, qm, 1)
    return x2.reshape(batch, seq, d)
```

```python
import functools

import numpy as np
import jax
import jax.numpy as jnp
from jax import lax
from jax.experimental import pallas as pl
from jax.experimental.pallas import tpu as pltpu

F32 = jnp.float32
BF16 = jnp.bfloat16
EPS = 1e-6

HG_HEADS = 6
HG_KDIM = 128
HG_VDIM = 128
HG_F = HG_HEADS * HG_KDIM
MLA_HEADS = 6
MLA_Q_LORA = 384
MLA_KV_LORA = 256
MLA_NOPE = 128
MLA_ROPE = 64
MLA_QK = MLA_NOPE + MLA_ROPE
MLA_VDIM = 128
MLA_PAD = 256
ROPE_THETA = 10000.0
MEM_HEADS = 4
MEM_HDIM = 64
MEM_W = MEM_HEADS * MEM_HDIM
MIX_W = 768
D_FF = 2816

LANES = 128
SUBLANES = 8
VMEM_LIMIT = 56 * 1024 * 1024

HG_CHUNK = 128
HG_LEVELS = 7
HG_BIG_LEVELS = 4
FFN_CHUNK = 256
LOG2E = 1.4426950408889634
FLASH_HEADS = 2
FLASH_TQ = 512
ROPE_ROWS = 512
MLA_SUB = 256


def _dot(a, b):
    return jnp.dot(a, b, preferred_element_type=F32)


def _dot_nt(a, b):
    return lax.dot_general(a, b, (((1,), (1,)), ((), ())), preferred_element_type=F32)


def _dot_tn(a, b):
    return lax.dot_general(a, b, (((0,), (0,)), ((), ())), preferred_element_type=F32)


def _rms(x, g):
    ms = jnp.mean(x * x, axis=-1, keepdims=True)
    return x * lax.rsqrt(ms + EPS) * g


def _sigmoid(x):
    return 1.0 / (1.0 + jnp.exp(-x))


def _split2(x):
    hi = x.astype(BF16)
    lo = (x - hi.astype(F32)).astype(BF16)
    return hi, lo


def _split3(x):
    hi = x.astype(BF16)
    r = x - hi.astype(F32)
    mid = r.astype(BF16)
    lo = (r - mid.astype(F32)).astype(BF16)
    return hi, mid, lo


def _const_spec(shape):
    nd = len(shape)
    return pl.BlockSpec(shape, lambda *_: (0,) * nd, pipeline_mode=pl.Buffered(1))


def _layer_spec(arr, layer):
    nd = arr.ndim - 1
    return pl.BlockSpec((None,) + arr.shape[1:], lambda *_: (layer,) + (0,) * nd,
                        pipeline_mode=pl.Buffered(1))


def _params(sem):
    return pltpu.CompilerParams(dimension_semantics=sem, vmem_limit_bytes=VMEM_LIMIT)


def _hg_inproj_kernel(x_ref, g_ref, w_ref, lbl_ref, qs_ref, lf_ref, kk_ref, v_ref, sg_ref, qm_ref):
    h = _rms(x_ref[...], g_ref[...]).astype(BF16)
    l = lbl_ref[...]
    e = jnp.exp(l - jnp.max(l, axis=0, keepdims=True))
    lb = e[0:1, :] / jnp.sum(e, axis=0, keepdims=True)
    q = _dot(h, w_ref[:, 0:HG_F])
    qs_ref[...] = q * _sigmoid(q) * (HG_KDIM ** -0.5)
    f = _dot(h, w_ref[:, HG_F:2 * HG_F])
    fg = lb + (1.0 - lb) * _sigmoid(f)
    lf_ref[...] = jnp.log2(fg)
    kk_ref[...] = 1.0 - fg
    v_ref[...] = _dot(h, w_ref[:, 2 * HG_F:3 * HG_F]).astype(BF16)
    gate = _dot(h, w_ref[:, 3 * HG_F:4 * HG_F])
    sg_ref[...] = gate * _sigmoid(gate)
    qm_ref[...] = _dot(h, w_ref[:, 4 * HG_F:4 * HG_F + MEM_W])


def _hg_inproj(x2, g, w, lb_logits, tm):
    n, d = x2.shape
    row = lambda i: (i, 0)
    out768 = pl.BlockSpec((tm, HG_F), row)
    return pl.pallas_call(
        _hg_inproj_kernel,
        grid=(n // tm,),
        in_specs=[pl.BlockSpec((tm, d), row), _const_spec((1, d)), _const_spec(w.shape),
                  _const_spec(lb_logits.shape)],
        out_specs=[out768, out768, out768, out768, out768, pl.BlockSpec((tm, MEM_W), row)],
        out_shape=[jax.ShapeDtypeStruct((n, HG_F), F32), jax.ShapeDtypeStruct((n, HG_F), F32),
                   jax.ShapeDtypeStruct((n, HG_F), F32), jax.ShapeDtypeStruct((n, HG_F), BF16),
                   jax.ShapeDtypeStruct((n, HG_F), F32), jax.ShapeDtypeStruct((n, MEM_W), F32)],
        compiler_params=_params(("arbitrary",)),
    )(x2, g, w, lb_logits)


def _memkv_kernel(mem_ref, g_ref, w_ref, kg_ref, grp_ref, k_ref, v_ref):
    mn = _rms(mem_ref[...], g_ref[...]).astype(BF16)
    kv = _dot(mn, w_ref[...])
    k = kv[:, :MEM_W]
    hi, lo = _split2(k * k)
    ms = _dot(hi, grp_ref[...]) + _dot(lo, grp_ref[...])
    k_ref[...] = (k * lax.rsqrt(ms + EPS) * kg_ref[...]).astype(BF16)
    v_ref[...] = kv[:, MEM_W:].astype(BF16)


def _memkv(mem, g, w, kg, grp):
    depth = g.shape[0]
    b, m, d = mem.shape
    spec_kv = pl.BlockSpec((None, None, m, MEM_W), lambda l, i: (l, i, 0, 0))
    return pl.pallas_call(
        _memkv_kernel,
        grid=(depth, b),
        in_specs=[pl.BlockSpec((None, m, d), lambda l, i: (i, 0, 0)),
                  pl.BlockSpec((None, 1, d), lambda l, i: (l, 0, 0)),
                  pl.BlockSpec((None, d, 2 * MEM_W), lambda l, i: (l, 0, 0)),
                  pl.BlockSpec((None, 1, MEM_W), lambda l, i: (l, 0, 0)),
                  _const_spec(grp.shape)],
        out_specs=[spec_kv, spec_kv],
        out_shape=[jax.ShapeDtypeStruct((depth, b, m, MEM_W), BF16)] * 2,
        compiler_params=_params(("arbitrary", "arbitrary")),
    )(mem, g, w, kg, grp)


def _hg_tables():
    C = HG_CHUNK
    t = np.arange(C)[:, None]
    u = np.arange(C)[None, :]
    small, sgn = [], []
    for l in range(HG_LEVELS):
        c = C >> (l + 1)
        mid = (t // (2 * c)) * (2 * c) + c
        upper = t >= mid
        if l < HG_BIG_LEVELS:
            sgn.append(np.broadcast_to(np.where(upper, 1.0, -1.0), (C, C)))
        else:
            small.append(np.where(upper, (u >= mid) & (u <= t), (u > t) & (u < mid)))
    tri = (u <= t).astype(np.float32)
    sums = np.concatenate([tri] + small, axis=0).astype(np.float32)
    sums = np.concatenate([sums, sums], axis=1)
    sgn = np.concatenate(sgn, axis=0).astype(np.float32)
    p = np.floor(np.log2(np.maximum(t ^ u, 1))).astype(np.int32)
    lev = np.where(t > u, HG_LEVELS - 1 - p, np.where(t == u, HG_LEVELS, -1)).astype(np.int32)
    return sums, tri, sgn, lev


def _hgrn_kernel(qs_ref, lf_ref, kk_ref, v_ref, sg_ref, sums_ref, tri_ref, sgn_ref, lev_ref, og_ref,
                 y_ref, st_ref):
    C = HG_CHUNK
    heads = range(HG_HEADS)
    cols = [slice(h * HG_KDIM, (h + 1) * HG_KDIM) for h in heads]

    @pl.when(pl.program_id(1) == 0)
    def _():
        st_ref[...] = jnp.zeros_like(st_ref)

    hi, mid, lo = _split3(lf_ref[...])
    d_all = _dot(sums_ref[...], jnp.concatenate([hi, mid], axis=0))
    b = d_all[0:C, :] + _dot(tri_ref[...], lo)
    d_small = d_all[C:, :]
    lev = lev_ref[...]
    masks = [lev == l for l in range(HG_LEVELS + 1)]

    pairs, inter, diag = [], [], []
    for h in heads:
        q = qs_ref[:, cols[h]]
        k = kk_ref[:, cols[h]]
        diag.append(jnp.sum(q * k, axis=-1, keepdims=True))
        qb = q.astype(BF16)
        kb = k.astype(BF16)
        bh = b[:, cols[h]]
        ops = []
        for l in range(HG_LEVELS):
            if l < HG_BIG_LEVELS:
                c = C >> (l + 1)
                ref = jnp.concatenate([jnp.broadcast_to(bh[i:i + 1, :], (2 * c, HG_KDIM))
                                       for i in range(c - 1, C, 2 * c)], axis=0)
                e = jnp.exp2((bh - ref) * sgn_ref[l * C:(l + 1) * C, :])
            else:
                e = jnp.exp2(d_small[(l - HG_BIG_LEVELS) * C:(l - HG_BIG_LEVELS + 1) * C, cols[h]])
            eb = e.astype(BF16)
            ops.append((qb * eb, kb * eb))
        pairs.append(ops)
        e_pre = jnp.exp2(bh)
        e_suf = jnp.exp2(bh[C - 1:C, :] - bh)
        inter.append((qb * e_pre.astype(BF16), kb * e_suf.astype(BF16), e_pre[C - 1:C, :]))

    scores = []
    for h in heads:
        a = jnp.where(masks[HG_LEVELS], diag[h], 0.0)
        for l in range(HG_LEVELS):
            a = jnp.where(masks[l], _dot_nt(*pairs[h][l]), a)
        scores.append(a.astype(BF16))

    for h in heads:
        qe, ke, dec = inter[h]
        v = v_ref[:, cols[h]]
        st = st_ref[h]
        o = _dot_nt(qe, st.astype(BF16)) + _dot(scores[h], v)
        st_ref[h] = st * dec + _dot_tn(v, ke)
        y_ref[:, cols[h]] = (_rms(o, og_ref[...]) * sg_ref[:, cols[h]]).astype(BF16)


def _hgrn(qs, lf, kk, v, sg, tables, og, batch, seq):
    n = qs.shape[0]
    nc = seq // HG_CHUNK
    row = lambda b, c: (b * nc + c, 0)
    spec = pl.BlockSpec((HG_CHUNK, HG_F), row)
    return pl.pallas_call(
        _hgrn_kernel,
        grid=(batch, nc),
        in_specs=[spec, spec, spec, spec, spec] + [_const_spec(t.shape) for t in tables]
                 + [_const_spec(og.shape)],
        out_specs=spec,
        out_shape=jax.ShapeDtypeStruct((n, HG_F), BF16),
        scratch_shapes=[pltpu.VMEM((HG_HEADS, HG_VDIM, HG_KDIM), F32)],
        compiler_params=_params(("arbitrary", "arbitrary")),
    )(qs, lf, kk, v, sg, *tables, og)


def _mix_out(x, y_ref, qm_ref, km_ref, vm_ref, qg_ref, grp_ref, w_ref):
    q = qm_ref[...]
    hi, lo = _split2(q * q)
    ms = _dot(hi, grp_ref[...]) + _dot(lo, grp_ref[...])
    qn = (q * lax.rsqrt(ms + EPS) * (qg_ref[...] * (MEM_HDIM ** -0.5 * LOG2E))).astype(BF16)
    tm = qn.shape[0]
    head = lax.broadcasted_iota(jnp.int32, qn.shape, 1) // MEM_HDIM
    qs = jnp.concatenate([jnp.where(head == h, qn, jnp.zeros_like(qn)) for h in range(MEM_HEADS)], axis=0)
    s = _dot_nt(qs, km_ref[...])
    p = jnp.exp2(s - jnp.max(s, axis=-1, keepdims=True))
    o = _dot(p.astype(BF16), vm_ref[...]) / jnp.sum(p, axis=-1, keepdims=True)
    m = o[0:tm, :]
    for h in range(1, MEM_HEADS):
        m = jnp.where(head == h, o[h * tm:(h + 1) * tm, :], m)
    return (x + _dot(y_ref[...], w_ref[0:MIX_W, :])
            + _dot(m.astype(BF16), w_ref[MIX_W:MIX_W + MEM_W, :]))


def _shift_row(t, prev):
    r = pltpu.roll(t, 1, axis=0)
    pr = pltpu.roll(prev, 1, axis=0)
    sub = lax.broadcasted_iota(jnp.int32, pr.shape, 0)
    head = jnp.where(sub < 1, pr, r[0:SUBLANES, :])
    return jnp.concatenate([head, r[SUBLANES:, :]], axis=0)


def _conv_ffn(x, g_ref, wup_ref, cw_ref, cb_ref, wdn_ref, carry_ref, act_ref):
    tm = x.shape[0]
    h = _rms(x, g_ref[...]).astype(BF16)

    def conv_slice(c0):
        cols = slice(c0, c0 + FFN_CHUNK)
        u = _dot(h, wup_ref[:, cols])
        prev = carry_ref[:, cols]
        carry_ref[:, cols] = u[tm - SUBLANES:, :]
        w0, w1, w2 = cw_ref[0:1, cols], cw_ref[1:2, cols], cw_ref[2:3, cols]
        t0, p0 = w0 * u, w0 * prev
        t1 = _shift_row(t0, p0) + w1 * u
        p1 = pltpu.roll(p0, 1, axis=0) + w1 * prev
        return _shift_row(t1, p1) + (w2 * u + cb_ref[:, cols])

    for c in range(D_FF // FFN_CHUNK):
        a = conv_slice(c * FFN_CHUNK).astype(BF16)
        b = conv_slice(D_FF + c * FFN_CHUNK).astype(BF16)
        sig = 1.0 / (1.0 + jnp.exp2(a * (-LOG2E)))
        act_ref[:, c * FFN_CHUNK:(c + 1) * FFN_CHUNK] = a * b * sig
    return x + _dot(act_ref[...], wdn_ref[...])


def _tail_kernel(x_ref, y_ref, qm_ref, km_ref, vm_ref, qg_ref, grp_ref, wout_ref,
                 g_ref, wup_ref, cw_ref, cb_ref, wdn_ref, o_ref, carry_ref, act_ref):
    @pl.when(pl.program_id(1) == 0)
    def _():
        carry_ref[...] = jnp.zeros_like(carry_ref)

    x1 = _mix_out(x_ref[...], y_ref, qm_ref, km_ref, vm_ref, qg_ref, grp_ref, wout_ref)
    o_ref[...] = _conv_ffn(x1, g_ref, wup_ref, cw_ref, cb_ref, wdn_ref, carry_ref, act_ref)


def _layer_tail(x2, y, qm, kmem, vmem, layer, qg, grp, wout, g, wup, cw, cb, wdn, batch, seq, tm):
    n, d = x2.shape
    nt = seq // tm
    row = lambda b, t: (b * nt + t, 0)
    mem_spec = pl.BlockSpec((None, None) + kmem.shape[2:], lambda b, t: (layer, b, 0, 0))
    return pl.pallas_call(
        _tail_kernel,
        grid=(batch, nt),
        in_specs=[pl.BlockSpec((tm, d), row), pl.BlockSpec((tm, MIX_W), row),
                  pl.BlockSpec((tm, MEM_W), row), mem_spec, mem_spec,
                  _layer_spec(qg, layer), _const_spec(grp.shape), _layer_spec(wout, layer)]
                 + [_layer_spec(p, layer) for p in (g, wup, cw, cb, wdn)],
        out_specs=pl.BlockSpec((tm, d), row),
        out_shape=jax.ShapeDtypeStruct((n, d), F32),
        scratch_shapes=[pltpu.VMEM((SUBLANES, 2 * D_FF), F32), pltpu.VMEM((tm, D_FF), BF16)],
        compiler_params=_params(("arbitrary", "arbitrary")),
    )(x2, y, qm, kmem, vmem, qg, grp, wout, g, wup, cw, cb, wdn)


def _rope_kernel(pos_ref, inv_ref, cos_ref, sin_ref):
    ang = pos_ref[...].astype(F32) * inv_ref[...]
    cos_ref[...] = jnp.cos(ang)
    sin_ref[...] = jnp.sin(ang)


def _rope_tables(positions):
    half = MLA_ROPE // 2
    per_row = LANES // half
    n = positions.size
    rows = n // per_row
    tr = min(ROPE_ROWS, rows)
    inv = ROPE_THETA ** (-jnp.arange(half, dtype=F32) / half)
    pos = jnp.broadcast_to(positions.reshape(n, 1), (n, half)).reshape(rows, LANES)
    spec = pl.BlockSpec((tr, LANES), lambda i: (i, 0))
    cos, sin = pl.pallas_call(
        _rope_kernel,
        grid=(rows // tr,),
        in_specs=[spec, _const_spec((1, LANES))],
        out_specs=[spec, spec],
        out_shape=[jax.ShapeDtypeStruct((rows, LANES), F32)] * 2,
        compiler_params=_params(("arbitrary",)),
    )(pos, jnp.tile(inv, per_row)[None, :])
    return jnp.concatenate([cos.reshape(n, half), sin.reshape(n, half)], axis=1)


def _rope_expand():
    half = MLA_ROPE // 2
    e = np.zeros((2 * half, 3 * LANES), np.float32)
    j = np.arange(half)
    for blk in range(2):
        e[j, blk * half + j] = 1.0
        e[half + j, (2 + blk) * half + j] = 1.0
    for blk in range(4):
        e[j, LANES + blk * half + j] = 1.0
        e[half + j, 2 * LANES + blk * half + j] = -1.0 if blk % 2 == 0 else 1.0
    return e


def _mla_prep_kernel(x_ref, cs_ref, g_ref, win_ref, exp_ref, qag_ref, kvag_ref, wuq_ref, wukv_ref,
                     qgn_ref, qgr_ref, kgn_ref, kgr_ref,
                     q_ref, k_ref, v_ref, qm_ref):
    subs = [slice(r * MLA_SUB, (r + 1) * MLA_SUB) for r in range(x_ref.shape[0] // MLA_SUB)]
    c0 = MLA_Q_LORA + MLA_KV_LORA
    zs = [_dot(_rms(x_ref[rows, :], g_ref[...]).astype(BF16), win_ref[...]) for rows in subs]
    qfs = [_dot(_rms(z[:, 0:MLA_Q_LORA], qag_ref[...]).astype(BF16), wuq_ref[...]) for z in zs]
    kvfs = [_dot(_rms(z[:, MLA_Q_LORA:c0], kvag_ref[...]).astype(BF16), wukv_ref[...]) for z in zs]

    for rows, z, qf, kvf in zip(subs, zs, qfs, kvfs):
        kpe2 = z[:, c0:c0 + LANES]
        qm_ref[rows, :] = z[:, c0 + LANES:]

        hi, lo = _split2(cs_ref[rows, :])
        rope = _dot(hi, exp_ref[...]) + _dot(lo, exp_ref[...])
        tq = rope[:, 0:LANES] * qgr_ref[...]
        cos = rope[:, LANES:2 * LANES]
        sin_signed = rope[:, 2 * LANES:3 * LANES]

        for hd in range(MLA_HEADS):
            qn = qf[:, hd * MLA_PAD:hd * MLA_PAD + MLA_NOPE]
            qr = qf[:, hd * MLA_PAD + MLA_NOPE:(hd + 1) * MLA_PAD]
            ss = jnp.sum(qn * qn, axis=-1, keepdims=True) + 0.5 * jnp.sum(qr * qr, axis=-1, keepdims=True)
            rs = lax.rsqrt(ss * (1.0 / MLA_QK) + EPS) * (MLA_QK ** -0.5 * LOG2E)
            q_ref[rows, hd * MLA_PAD:hd * MLA_PAD + MLA_NOPE] = (qn * rs * qgn_ref[...]).astype(BF16)
            q_ref[rows, hd * MLA_PAD + MLA_NOPE:(hd + 1) * MLA_PAD] = (qr * rs * tq).astype(BF16)

        pk = kpe2 * kgr_ref[...]
        kr = pk * cos + pltpu.roll(pk, MLA_ROPE // 2, axis=1) * sin_signed
        sk = 0.5 * jnp.sum(kpe2 * kpe2, axis=-1, keepdims=True)
        for hd in range(MLA_HEADS):
            kn = kvf[:, hd * MLA_PAD:hd * MLA_PAD + MLA_NOPE]
            ss = jnp.sum(kn * kn, axis=-1, keepdims=True) + sk
            rs = lax.rsqrt(ss * (1.0 / MLA_QK) + EPS)
            k_ref[rows, hd * MLA_PAD:hd * MLA_PAD + MLA_NOPE] = (kn * rs * kgn_ref[...]).astype(BF16)
            k_ref[rows, hd * MLA_PAD + MLA_NOPE:(hd + 1) * MLA_PAD] = (kr * rs).astype(BF16)
            v_ref[rows, hd * MLA_VDIM:(hd + 1) * MLA_VDIM] = kvf[:, hd * MLA_PAD + MLA_NOPE:(hd + 1) * MLA_PAD].astype(BF16)


def _mla_prep(x2, cs, g, win, expand, qag, kvag, wuq, wukv, tabs, tm):
    n, d = x2.shape
    row = lambda i: (i, 0)
    consts = [g, win, expand, qag, kvag, wuq, wukv] + list(tabs)
    return pl.pallas_call(
        _mla_prep_kernel,
        grid=(n // tm,),
        in_specs=[pl.BlockSpec((tm, d), row), pl.BlockSpec((tm, MLA_ROPE), row)]
                 + [_const_spec(c.shape) for c in consts],
        out_specs=[pl.BlockSpec((tm, MLA_HEADS * MLA_PAD), row), pl.BlockSpec((tm, MLA_HEADS * MLA_PAD), row),
                   pl.BlockSpec((tm, MLA_HEADS * MLA_VDIM), row), pl.BlockSpec((tm, MEM_W), row)],
        out_shape=[jax.ShapeDtypeStruct((n, MLA_HEADS * MLA_PAD), BF16),
                   jax.ShapeDtypeStruct((n, MLA_HEADS * MLA_PAD), BF16),
                   jax.ShapeDtypeStruct((n, MLA_HEADS * MLA_VDIM), BF16),
                   jax.ShapeDtypeStruct((n, MEM_W), F32)],
        compiler_params=_params(("arbitrary",)),
    )(x2, cs, *consts)


def _flash_kernel(q_ref, k_ref, v_ref, o_ref):
    tq = q_ref.shape[0]
    n_tiles = k_ref.shape[0] // tq
    r = lax.broadcasted_iota(jnp.int32, (tq, tq), 0)
    c = lax.broadcasted_iota(jnp.int32, (tq, tq), 1)

    def attend(qi):
        past = qi * tq
        heads = range(FLASH_HEADS)
        kc = [slice(h * MLA_PAD, (h + 1) * MLA_PAD) for h in heads]
        vc = [slice(h * MLA_VDIM, (h + 1) * MLA_VDIM) for h in heads]
        sd = [_dot_nt(q_ref[:, kc[h]], k_ref[past:past + tq, kc[h]]) for h in heads]
        sp = [_dot_nt(q_ref[:, kc[h]], k_ref[0:past, kc[h]]) for h in heads] if qi > 0 else None
        pd, pp, ls = [], [], []
        for h in heads:
            s = jnp.where(c <= r, sd[h], -jnp.inf)
            m = jnp.max(s, axis=-1, keepdims=True)
            if qi > 0:
                m = jnp.maximum(m, jnp.max(sp[h], axis=-1, keepdims=True))
            p = jnp.exp2(s - m)
            l = jnp.sum(p, axis=-1, keepdims=True)
            pd.append(p.astype(BF16))
            if qi > 0:
                p = jnp.exp2(sp[h] - m)
                l = l + jnp.sum(p, axis=-1, keepdims=True)
                pp.append(p.astype(BF16))
            ls.append(l)
        for h in heads:
            acc = _dot(pd[h], v_ref[past:past + tq, vc[h]])
            if qi > 0:
                acc = acc + _dot(pp[h], v_ref[0:past, vc[h]])
            o_ref[:, vc[h]] = (acc / ls[h]).astype(BF16)

    for qi in range(n_tiles):
        pl.when(pl.program_id(2) == qi)(functools.partial(attend, qi))


def _flash(q, k, v, batch, seq, tq):
    q3 = q.reshape(batch, seq, MLA_HEADS * MLA_PAD)
    k3 = k.reshape(batch, seq, MLA_HEADS * MLA_PAD)
    v3 = v.reshape(batch, seq, MLA_HEADS * MLA_VDIM)
    wq = FLASH_HEADS * MLA_PAD
    wv = FLASH_HEADS * MLA_VDIM
    out = pl.pallas_call(
        _flash_kernel,
        grid=(batch, MLA_HEADS // FLASH_HEADS, seq // tq),
        in_specs=[pl.BlockSpec((None, tq, wq), lambda b, h, i: (b, i, h)),
                  pl.BlockSpec((None, seq, wq), lambda b, h, i: (b, 0, h)),
                  pl.BlockSpec((None, seq, wv), lambda b, h, i: (b, 0, h))],
        out_specs=pl.BlockSpec((None, tq, wv), lambda b, h, i: (b, i, h)),
        out_shape=jax.ShapeDtypeStruct((batch, seq, MLA_HEADS * MLA_VDIM), BF16),
        compiler_params=_params(("arbitrary", "arbitrary", "arbitrary")),
    )(q3, k3, v3)
    return out.reshape(batch * seq, MLA_HEADS * MLA_VDIM)


def _mla_layouts(w_in, w_uq, q_norm_g, k_norm_g):
    half = MLA_ROPE // 2
    c0 = MLA_Q_LORA + MLA_KV_LORA
    kpe = w_in[:, c0:c0 + MLA_ROPE]
    win = jnp.concatenate([w_in[:, :c0], kpe, kpe, w_in[:, c0 + MLA_ROPE:]], axis=1)
    wq = w_uq.reshape(MLA_Q_LORA, MLA_HEADS, MLA_QK)
    x1 = wq[:, :, MLA_NOPE:MLA_NOPE + half]
    x2 = wq[:, :, MLA_NOPE + half:]
    wuq = jnp.concatenate([wq[:, :, :MLA_NOPE], x1, x2, x2, x1], axis=2).reshape(MLA_Q_LORA, MLA_HEADS * MLA_PAD)
    g1, g2 = q_norm_g[MLA_NOPE:MLA_NOPE + half], q_norm_g[MLA_NOPE + half:]
    qgn = q_norm_g[:MLA_NOPE][None, :]
    qgr = jnp.concatenate([g1, g2, -g2, g1])[None, :]
    kgn = k_norm_g[:MLA_NOPE][None, :]
    kgr = jnp.concatenate([k_norm_g[MLA_NOPE:], k_norm_g[MLA_NOPE:]])[None, :]
    return win, wuq, (qgn, qgr, kgn, kgr)


def kernel(x, mem, positions, mix_norm_g, ffn_norm_g, mem_norm_g, w_mem_kv, mem_q_norm_g, mem_k_norm_g, w_out, w_up, conv_w, conv_b, w_down, hg_w_in, hg_lb_logits, hg_out_norm_g, mla_w_in, mla_qa_norm_g, mla_kva_norm_g, mla_w_uq, mla_w_ukv, mla_q_norm_g, mla_k_norm_g):
    batch, seq, d = x.shape
    n = batch * seq
    tm_big = min(512, seq)
    x2 = x.reshape(n, d)

    grp = jnp.asarray(np.kron(np.eye(MEM_HEADS), np.full((MEM_HDIM, MEM_HDIM), 1.0 / MEM_HDIM)), BF16)
    sums, tri, sgn, lev = _hg_tables()
    hg_tables = (jnp.asarray(sums, BF16), jnp.asarray(tri, BF16), jnp.asarray(sgn), jnp.asarray(lev))

    kmem, vmem = _memkv(mem, mem_norm_g[:, None, :], w_mem_kv.astype(BF16),
                        jnp.tile(mem_k_norm_g, (1, MEM_HEADS))[:, None, :], grp)
    qg = jnp.tile(mem_q_norm_g, (1, MEM_HEADS))[:, None, :]
    w_out_b, w_up_b, w_down_b = w_out.astype(BF16), w_up.astype(BF16), w_down.astype(BF16)

    def finish_layer(xin, y, qm, layer):
        return _layer_tail(xin, y, qm, kmem, vmem, layer, qg, grp, w_out_b, ffn_norm_g[:, None, :],
                           w_up_b, conv_w, conv_b[:, None, :], w_down_b, batch, seq, tm_big)

    qs, lf, kk, v, sg, qm = _hg_inproj(x2, mix_norm_g[0][None, :], hg_w_in[0].astype(BF16),
                                        hg_lb_logits, tm_big)
    y = _hgrn(qs, lf, kk, v, sg, hg_tables, hg_out_norm_g[0][None, :], batch, seq)
    x2 = finish_layer(x2, y, qm, 0)

    win, wuq, tabs = _mla_layouts(mla_w_in[0], mla_w_uq[0], mla_q_norm_g[0], mla_k_norm_g[0])
    q, k, v, qm = _mla_prep(x2, _rope_tables(positions), mix_norm_g[1][None, :], win.astype(BF16),
                            jnp.asarray(_rope_expand(), BF16),
                            mla_qa_norm_g[0][None, :], mla_kva_norm_g[0][None, :], wuq.astype(BF16),
                            mla_w_ukv[0].astype(BF16), tabs, tm_big)
    y = _flash(q, k, v, batch, seq, min(FLASH_TQ, seq))
    x2 = finish_layer(x2, y, qm, 1)
    return x2.reshape(batch, seq, d)
```

```python
import functools

import numpy as np
import jax
import jax.numpy as jnp
from jax import lax
from jax.experimental import pallas as pl
from jax.experimental.pallas import tpu as pltpu

F32 = jnp.float32
BF16 = jnp.bfloat16
EPS = 1e-6

HG_HEADS = 6
HG_KDIM = 128
HG_VDIM = 128
HG_F = HG_HEADS * HG_KDIM
MLA_HEADS = 6
MLA_Q_LORA = 384
MLA_KV_LORA = 256
MLA_NOPE = 128
MLA_ROPE = 64
MLA_QK = MLA_NOPE + MLA_ROPE
MLA_VDIM = 128
MLA_PAD = 256
ROPE_THETA = 10000.0
MEM_HEADS = 4
MEM_HDIM = 64
MEM_W = MEM_HEADS * MEM_HDIM
MIX_W = 768
D_FF = 2816

LANES = 128
SUBLANES = 8
VMEM_LIMIT = 56 * 1024 * 1024

HG_CHUNK = 128
HG_LEVELS = 7
HG_BIG_LEVELS = 4
HG_STEP_CHUNKS = 4
FFN_CHUNK = 256
LOG2E = 1.4426950408889634
FLASH_HEADS = 2
FLASH_TQ = 512
ROPE_ROWS = 512
MLA_SUB = 256


def _dot(a, b):
    return jnp.dot(a, b, preferred_element_type=F32)


def _dot_nt(a, b):
    return lax.dot_general(a, b, (((1,), (1,)), ((), ())), preferred_element_type=F32)


def _dot_tn(a, b):
    return lax.dot_general(a, b, (((0,), (0,)), ((), ())), preferred_element_type=F32)


def _rms(x, g):
    ms = jnp.mean(x * x, axis=-1, keepdims=True)
    return x * lax.rsqrt(ms + EPS) * g


def _sigmoid(x):
    return 1.0 / (1.0 + jnp.exp(-x))


def _split2(x):
    hi = x.astype(BF16)
    lo = (x - hi.astype(F32)).astype(BF16)
    return hi, lo


def _split3(x):
    hi = x.astype(BF16)
    r = x - hi.astype(F32)
    mid = r.astype(BF16)
    lo = (r - mid.astype(F32)).astype(BF16)
    return hi, mid, lo


def _const_spec(shape):
    nd = len(shape)
    return pl.BlockSpec(shape, lambda *_: (0,) * nd, pipeline_mode=pl.Buffered(1))


def _layer_spec(arr, layer):
    nd = arr.ndim - 1
    return pl.BlockSpec((None,) + arr.shape[1:], lambda *_: (layer,) + (0,) * nd,
                        pipeline_mode=pl.Buffered(1))


def _params(sem):
    return pltpu.CompilerParams(dimension_semantics=sem, vmem_limit_bytes=VMEM_LIMIT)


def _hg_inproj_kernel(x_ref, g_ref, w_ref, lbl_ref, qs_ref, lf_ref, kk_ref, v_ref, sg_ref, qm_ref):
    h = _rms(x_ref[...], g_ref[...]).astype(BF16)
    l = lbl_ref[...]
    e = jnp.exp(l - jnp.max(l, axis=0, keepdims=True))
    lb = e[0:1, :] / jnp.sum(e, axis=0, keepdims=True)
    q = _dot(h, w_ref[:, 0:HG_F])
    qs_ref[...] = q * _sigmoid(q) * (HG_KDIM ** -0.5)
    f = _dot(h, w_ref[:, HG_F:2 * HG_F])
    fg = lb + (1.0 - lb) * _sigmoid(f)
    lf_ref[...] = jnp.log2(fg)
    kk_ref[...] = 1.0 - fg
    v_ref[...] = _dot(h, w_ref[:, 2 * HG_F:3 * HG_F]).astype(BF16)
    gate = _dot(h, w_ref[:, 3 * HG_F:4 * HG_F])
    sg_ref[...] = gate * _sigmoid(gate)
    qm_ref[...] = _dot(h, w_ref[:, 4 * HG_F:4 * HG_F + MEM_W])


def _hg_inproj(x2, g, w, lb_logits, tm):
    n, d = x2.shape
    row = lambda i: (i, 0)
    out768 = pl.BlockSpec((tm, HG_F), row)
    return pl.pallas_call(
        _hg_inproj_kernel,
        grid=(n // tm,),
        in_specs=[pl.BlockSpec((tm, d), row), _const_spec((1, d)), _const_spec(w.shape),
                  _const_spec(lb_logits.shape)],
        out_specs=[out768, out768, out768, out768, out768, pl.BlockSpec((tm, MEM_W), row)],
        out_shape=[jax.ShapeDtypeStruct((n, HG_F), F32), jax.ShapeDtypeStruct((n, HG_F), F32),
                   jax.ShapeDtypeStruct((n, HG_F), F32), jax.ShapeDtypeStruct((n, HG_F), BF16),
                   jax.ShapeDtypeStruct((n, HG_F), F32), jax.ShapeDtypeStruct((n, MEM_W), F32)],
        compiler_params=_params(("arbitrary",)),
    )(x2, g, w, lb_logits)


def _memkv_kernel(mem_ref, g_ref, w_ref, kg_ref, grp_ref, k_ref, v_ref):
    mn = _rms(mem_ref[...], g_ref[...]).astype(BF16)
    kv = _dot(mn, w_ref[...])
    k = kv[:, :MEM_W]
    hi, lo = _split2(k * k)
    ms = _dot(hi, grp_ref[...]) + _dot(lo, grp_ref[...])
    k_ref[...] = (k * lax.rsqrt(ms + EPS) * kg_ref[...]).astype(BF16)
    v_ref[...] = kv[:, MEM_W:].astype(BF16)


def _memkv(mem, g, w, kg, grp):
    depth = g.shape[0]
    b, m, d = mem.shape
    spec_kv = pl.BlockSpec((None, None, m, MEM_W), lambda l, i: (l, i, 0, 0))
    return pl.pallas_call(
        _memkv_kernel,
        grid=(depth, b),
        in_specs=[pl.BlockSpec((None, m, d), lambda l, i: (i, 0, 0)),
                  pl.BlockSpec((None, 1, d), lambda l, i: (l, 0, 0)),
                  pl.BlockSpec((None, d, 2 * MEM_W), lambda l, i: (l, 0, 0)),
                  pl.BlockSpec((None, 1, MEM_W), lambda l, i: (l, 0, 0)),
                  _const_spec(grp.shape)],
        out_specs=[spec_kv, spec_kv],
        out_shape=[jax.ShapeDtypeStruct((depth, b, m, MEM_W), BF16)] * 2,
        compiler_params=_params(("arbitrary", "arbitrary")),
    )(mem, g, w, kg, grp)


def _hg_tables():
    C = HG_CHUNK
    t = np.arange(C)[:, None]
    u = np.arange(C)[None, :]
    small = []
    for l in range(HG_BIG_LEVELS, HG_LEVELS):
        c = C >> (l + 1)
        mid = (t // (2 * c)) * (2 * c) + c
        small.append(np.where(t >= mid, (u >= mid) & (u <= t), (u > t) & (u < mid)))
    tri = (u <= t).astype(np.float32)
    sums = np.concatenate([tri] + small, axis=0).astype(np.float32)
    sums = np.concatenate([sums, sums], axis=1)
    p = np.floor(np.log2(np.maximum(t ^ u, 1))).astype(np.int32)
    lev = np.where(t > u, HG_LEVELS - 1 - p, np.where(t == u, HG_LEVELS, -1)).astype(np.int32)
    return sums, tri, lev


def _hgrn_kernel(qs_ref, lf_ref, kk_ref, v_ref, sg_ref, sums_ref, tri_ref, lev_ref, og_ref, y_ref, st_ref):
    C = HG_CHUNK
    heads = range(HG_HEADS)
    cols = [slice(h * HG_KDIM, (h + 1) * HG_KDIM) for h in heads]
    chunks = [slice(s * C, (s + 1) * C) for s in range(qs_ref.shape[0] // C)]

    @pl.when(pl.program_id(1) == 0)
    def _():
        st_ref[...] = jnp.zeros_like(st_ref)

    lev = lev_ref[...]
    masks = [lev == l for l in range(HG_LEVELS + 1)]

    pairs, inter, diag = {}, {}, {}
    for s, rows in enumerate(chunks):
        hi, mid, lo = _split3(lf_ref[rows, :])
        d_all = _dot(sums_ref[...], jnp.concatenate([hi, mid], axis=0))
        b = d_all[0:C, :] + _dot(tri_ref[...], lo)
        d_small = d_all[C:, :]
        for h in heads:
            q = qs_ref[rows, cols[h]]
            k = kk_ref[rows, cols[h]]
            diag[s, h] = jnp.sum(q * k, axis=-1, keepdims=True)
            qb = q.astype(BF16)
            kb = k.astype(BF16)
            bh = b[:, cols[h]]
            ops = []
            for l in range(HG_LEVELS):
                if l < HG_BIG_LEVELS:
                    c = C >> (l + 1)
                    parts = []
                    for lo_row in range(0, C, 2 * c):
                        r = jnp.broadcast_to(bh[lo_row + c - 1:lo_row + c, :], (c, HG_KDIM))
                        parts += [r - bh[lo_row:lo_row + c, :], bh[lo_row + c:lo_row + 2 * c, :] - r]
                    e = jnp.exp2(jnp.concatenate(parts, axis=0))
                else:
                    e = jnp.exp2(d_small[(l - HG_BIG_LEVELS) * C:(l - HG_BIG_LEVELS + 1) * C, cols[h]])
                eb = e.astype(BF16)
                ops.append((qb * eb, kb * eb))
            pairs[s, h] = ops
            e_pre = jnp.exp2(bh)
            e_suf = jnp.exp2(bh[C - 1:C, :] - bh)
            inter[s, h] = (qb * e_pre.astype(BF16), kb * e_suf.astype(BF16), e_pre[C - 1:C, :])

    scores = {}
    for s in range(len(chunks)):
        for h in heads:
            a = jnp.where(masks[HG_LEVELS], diag[s, h], 0.0)
            for l in range(HG_LEVELS):
                a = jnp.where(masks[l], _dot_nt(*pairs[s, h][l]), a)
            scores[s, h] = a.astype(BF16)

    for s, rows in enumerate(chunks):
        for h in heads:
            qe, ke, dec = inter[s, h]
            v = v_ref[rows, cols[h]]
            st = st_ref[h]
            o = _dot_nt(qe, st.astype(BF16)) + _dot(scores[s, h], v)
            st_ref[h] = st * dec + _dot_tn(v, ke)
            y_ref[rows, cols[h]] = (_rms(o, og_ref[...]) * sg_ref[rows, cols[h]]).astype(BF16)


def _hgrn(qs, lf, kk, v, sg, tables, og, batch, seq):
    n = qs.shape[0]
    step = HG_CHUNK * HG_STEP_CHUNKS
    ns = seq // step
    row = lambda b, c: (b * ns + c, 0)
    spec = pl.BlockSpec((step, HG_F), row)
    return pl.pallas_call(
        _hgrn_kernel,
        grid=(batch, ns),
        in_specs=[spec, spec, spec, spec, spec] + [_const_spec(t.shape) for t in tables]
                 + [_const_spec(og.shape)],
        out_specs=spec,
        out_shape=jax.ShapeDtypeStruct((n, HG_F), BF16),
        scratch_shapes=[pltpu.VMEM((HG_HEADS, HG_VDIM, HG_KDIM), F32)],
        compiler_params=_params(("arbitrary", "arbitrary")),
    )(qs, lf, kk, v, sg, *tables, og)


def _mix_out(x, y_ref, qm_ref, km_ref, vm_ref, qg_ref, grp_ref, w_ref):
    q = qm_ref[...]
    hi, lo = _split2(q * q)
    ms = _dot(hi, grp_ref[...]) + _dot(lo, grp_ref[...])
    qn = (q * lax.rsqrt(ms + EPS) * (qg_ref[...] * (MEM_HDIM ** -0.5 * LOG2E))).astype(BF16)
    tm = qn.shape[0]
    head = lax.broadcasted_iota(jnp.int32, qn.shape, 1) // MEM_HDIM
    qs = jnp.concatenate([jnp.where(head == h, qn, jnp.zeros_like(qn)) for h in range(MEM_HEADS)], axis=0)
    s = _dot_nt(qs, km_ref[...])
    p = jnp.exp2(s - jnp.max(s, axis=-1, keepdims=True))
    o = _dot(p.astype(BF16), vm_ref[...]) / jnp.sum(p, axis=-1, keepdims=True)
    m = o[0:tm, :]
    for h in range(1, MEM_HEADS):
        m = jnp.where(head == h, o[h * tm:(h + 1) * tm, :], m)
    return (x + _dot(y_ref[...], w_ref[0:MIX_W, :])
            + _dot(m.astype(BF16), w_ref[MIX_W:MIX_W + MEM_W, :]))


def _shift_row(t, prev):
    r = pltpu.roll(t, 1, axis=0)
    pr = pltpu.roll(prev, 1, axis=0)
    sub = lax.broadcasted_iota(jnp.int32, pr.shape, 0)
    head = jnp.where(sub < 1, pr, r[0:SUBLANES, :])
    return jnp.concatenate([head, r[SUBLANES:, :]], axis=0)


def _conv_ffn(x, g_ref, wup_ref, cw_ref, cb_ref, wdn_ref, carry_ref, act_ref):
    tm = x.shape[0]
    h = _rms(x, g_ref[...]).astype(BF16)

    def conv_slice(c0):
        cols = slice(c0, c0 + FFN_CHUNK)
        u = _dot(h, wup_ref[:, cols])
        prev = carry_ref[:, cols]
        carry_ref[:, cols] = u[tm - SUBLANES:, :]
        w0, w1, w2 = cw_ref[0:1, cols], cw_ref[1:2, cols], cw_ref[2:3, cols]
        t0, p0 = w0 * u, w0 * prev
        t1 = _shift_row(t0, p0) + w1 * u
        p1 = pltpu.roll(p0, 1, axis=0) + w1 * prev
        return _shift_row(t1, p1) + (w2 * u + cb_ref[:, cols])

    for c in range(D_FF // FFN_CHUNK):
        a = conv_slice(c * FFN_CHUNK).astype(BF16)
        b = conv_slice(D_FF + c * FFN_CHUNK).astype(BF16)
        sig = 1.0 / (1.0 + jnp.exp2(a * (-LOG2E)))
        act_ref[:, c * FFN_CHUNK:(c + 1) * FFN_CHUNK] = a * b * sig
    return x + _dot(act_ref[...], wdn_ref[...])


def _tail_kernel(x_ref, y_ref, qm_ref, km_ref, vm_ref, qg_ref, grp_ref, wout_ref,
                 g_ref, wup_ref, cw_ref, cb_ref, wdn_ref, o_ref, carry_ref, act_ref):
    @pl.when(pl.program_id(1) == 0)
    def _():
        carry_ref[...] = jnp.zeros_like(carry_ref)

    x1 = _mix_out(x_ref[...], y_ref, qm_ref, km_ref, vm_ref, qg_ref, grp_ref, wout_ref)
    o_ref[...] = _conv_ffn(x1, g_ref, wup_ref, cw_ref, cb_ref, wdn_ref, carry_ref, act_ref)


def _layer_tail(x2, y, qm, kmem, vmem, layer, qg, grp, wout, g, wup, cw, cb, wdn, batch, seq, tm):
    n, d = x2.shape
    nt = seq // tm
    row = lambda b, t: (b * nt + t, 0)
    mem_spec = pl.BlockSpec((None, None) + kmem.shape[2:], lambda b, t: (layer, b, 0, 0))
    return pl.pallas_call(
        _tail_kernel,
        grid=(batch, nt),
        in_specs=[pl.BlockSpec((tm, d), row), pl.BlockSpec((tm, MIX_W), row),
                  pl.BlockSpec((tm, MEM_W), row), mem_spec, mem_spec,
                  _layer_spec(qg, layer), _const_spec(grp.shape), _layer_spec(wout, layer)]
                 + [_layer_spec(p, layer) for p in (g, wup, cw, cb, wdn)],
        out_specs=pl.BlockSpec((tm, d), row),
        out_shape=jax.ShapeDtypeStruct((n, d), F32),
        scratch_shapes=[pltpu.VMEM((SUBLANES, 2 * D_FF), F32), pltpu.VMEM((tm, D_FF), BF16)],
        compiler_params=_params(("arbitrary", "arbitrary")),
    )(x2, y, qm, kmem, vmem, qg, grp, wout, g, wup, cw, cb, wdn)


def _rope_kernel(pos_ref, inv_ref, cos_ref, sin_ref):
    ang = pos_ref[...].astype(F32) * inv_ref[...]
    cos_ref[...] = jnp.cos(ang)
    sin_ref[...] = jnp.sin(ang)


def _rope_tables(positions):
    half = MLA_ROPE // 2
    per_row = LANES // half
    n = positions.size
    rows = n // per_row
    tr = min(ROPE_ROWS, rows)
    inv = ROPE_THETA ** (-jnp.arange(half, dtype=F32) / half)
    pos = jnp.broadcast_to(positions.reshape(n, 1), (n, half)).reshape(rows, LANES)
    spec = pl.BlockSpec((tr, LANES), lambda i: (i, 0))
    cos, sin = pl.pallas_call(
        _rope_kernel,
        grid=(rows // tr,),
        in_specs=[spec, _const_spec((1, LANES))],
        out_specs=[spec, spec],
        out_shape=[jax.ShapeDtypeStruct((rows, LANES), F32)] * 2,
        compiler_params=_params(("arbitrary",)),
    )(pos, jnp.tile(inv, per_row)[None, :])
    return jnp.concatenate([cos.reshape(n, half), sin.reshape(n, half)], axis=1)


def _rope_expand():
    half = MLA_ROPE // 2
    e = np.zeros((2 * half, 3 * LANES), np.float32)
    j = np.arange(half)
    for blk in range(2):
        e[j, blk * half + j] = 1.0
        e[half + j, (2 + blk) * half + j] = 1.0
    for blk in range(4):
        e[j, LANES + blk * half + j] = 1.0
        e[half + j, 2 * LANES + blk * half + j] = -1.0 if blk % 2 == 0 else 1.0
    return e


def _mla_prep_kernel(x_ref, cs_ref, g_ref, win_ref, exp_ref, qag_ref, kvag_ref, wuq_ref, wukv_ref,
                     qgn_ref, qgr_ref, kgn_ref, kgr_ref,
                     q_ref, k_ref, v_ref, qm_ref):
    subs = [slice(r * MLA_SUB, (r + 1) * MLA_SUB) for r in range(x_ref.shape[0] // MLA_SUB)]
    c0 = MLA_Q_LORA + MLA_KV_LORA
    zs = [_dot(_rms(x_ref[rows, :], g_ref[...]).astype(BF16), win_ref[...]) for rows in subs]
    qfs = [_dot(_rms(z[:, 0:MLA_Q_LORA], qag_ref[...]).astype(BF16), wuq_ref[...]) for z in zs]
    kvfs = [_dot(_rms(z[:, MLA_Q_LORA:c0], kvag_ref[...]).astype(BF16), wukv_ref[...]) for z in zs]

    for rows, z, qf, kvf in zip(subs, zs, qfs, kvfs):
        kpe2 = z[:, c0:c0 + LANES]
        qm_ref[rows, :] = z[:, c0 + LANES:]

        hi, lo = _split2(cs_ref[rows, :])
        rope = _dot(hi, exp_ref[...]) + _dot(lo, exp_ref[...])
        tq = rope[:, 0:LANES] * qgr_ref[...]
        cos = rope[:, LANES:2 * LANES]
        sin_signed = rope[:, 2 * LANES:3 * LANES]

        for hd in range(MLA_HEADS):
            qn = qf[:, hd * MLA_PAD:hd * MLA_PAD + MLA_NOPE]
            qr = qf[:, hd * MLA_PAD + MLA_NOPE:(hd + 1) * MLA_PAD]
            ss = jnp.sum(qn * qn, axis=-1, keepdims=True) + 0.5 * jnp.sum(qr * qr, axis=-1, keepdims=True)
            rs = lax.rsqrt(ss * (1.0 / MLA_QK) + EPS) * (MLA_QK ** -0.5 * LOG2E)
            q_ref[rows, hd * MLA_PAD:hd * MLA_PAD + MLA_NOPE] = (qn * rs * qgn_ref[...]).astype(BF16)
            q_ref[rows, hd * MLA_PAD + MLA_NOPE:(hd + 1) * MLA_PAD] = (qr * rs * tq).astype(BF16)

        pk = kpe2 * kgr_ref[...]
        kr = pk * cos + pltpu.roll(pk, MLA_ROPE // 2, axis=1) * sin_signed
        sk = 0.5 * jnp.sum(kpe2 * kpe2, axis=-1, keepdims=True)
        for hd in range(MLA_HEADS):
            kn = kvf[:, hd * MLA_PAD:hd * MLA_PAD + MLA_NOPE]
            ss = jnp.sum(kn * kn, axis=-1, keepdims=True) + sk
            rs = lax.rsqrt(ss * (1.0 / MLA_QK) + EPS)
            k_ref[rows, hd * MLA_PAD:hd * MLA_PAD + MLA_NOPE] = (kn * rs * kgn_ref[...]).astype(BF16)
            k_ref[rows, hd * MLA_PAD + MLA_NOPE:(hd + 1) * MLA_PAD] = (kr * rs).astype(BF16)
            v_ref[rows, hd * MLA_VDIM:(hd + 1) * MLA_VDIM] = kvf[:, hd * MLA_PAD + MLA_NOPE:(hd + 1) * MLA_PAD].astype(BF16)


def _mla_prep(x2, cs, g, win, expand, qag, kvag, wuq, wukv, tabs, tm):
    n, d = x2.shape
    row = lambda i: (i, 0)
    consts = [g, win, expand, qag, kvag, wuq, wukv] + list(tabs)
    return pl.pallas_call(
        _mla_prep_kernel,
        grid=(n // tm,),
        in_specs=[pl.BlockSpec((tm, d), row), pl.BlockSpec((tm, MLA_ROPE), row)]
                 + [_const_spec(c.shape) for c in consts],
        out_specs=[pl.BlockSpec((tm, MLA_HEADS * MLA_PAD), row), pl.BlockSpec((tm, MLA_HEADS * MLA_PAD), row),
                   pl.BlockSpec((tm, MLA_HEADS * MLA_VDIM), row), pl.BlockSpec((tm, MEM_W), row)],
        out_shape=[jax.ShapeDtypeStruct((n, MLA_HEADS * MLA_PAD), BF16),
                   jax.ShapeDtypeStruct((n, MLA_HEADS * MLA_PAD), BF16),
                   jax.ShapeDtypeStruct((n, MLA_HEADS * MLA_VDIM), BF16),
                   jax.ShapeDtypeStruct((n, MEM_W), F32)],
        compiler_params=_params(("arbitrary",)),
    )(x2, cs, *consts)


def _flash_kernel(q_ref, k_ref, v_ref, o_ref):
    tq = q_ref.shape[0]
    n_tiles = k_ref.shape[0] // tq
    r = lax.broadcasted_iota(jnp.int32, (tq, tq), 0)
    c = lax.broadcasted_iota(jnp.int32, (tq, tq), 1)

    def attend(qi):
        past = qi * tq
        heads = range(FLASH_HEADS)
        kc = [slice(h * MLA_PAD, (h + 1) * MLA_PAD) for h in heads]
        vc = [slice(h * MLA_VDIM, (h + 1) * MLA_VDIM) for h in heads]
        sd = [_dot_nt(q_ref[:, kc[h]], k_ref[past:past + tq, kc[h]]) for h in heads]
        sp = [_dot_nt(q_ref[:, kc[h]], k_ref[0:past, kc[h]]) for h in heads] if qi > 0 else None
        pd, pp, ls = [], [], []
        for h in heads:
            s = jnp.where(c <= r, sd[h], -jnp.inf)
            m = jnp.max(s, axis=-1, keepdims=True)
            if qi > 0:
                m = jnp.maximum(m, jnp.max(sp[h], axis=-1, keepdims=True))
            p = jnp.exp2(s - m)
            l = jnp.sum(p, axis=-1, keepdims=True)
            pd.append(p.astype(BF16))
            if qi > 0:
                p = jnp.exp2(sp[h] - m)
                l = l + jnp.sum(p, axis=-1, keepdims=True)
                pp.append(p.astype(BF16))
            ls.append(l)
        for h in heads:
            acc = _dot(pd[h], v_ref[past:past + tq, vc[h]])
            if qi > 0:
                acc = acc + _dot(pp[h], v_ref[0:past, vc[h]])
            o_ref[:, vc[h]] = (acc / ls[h]).astype(BF16)

    for qi in range(n_tiles):
        pl.when(pl.program_id(2) == qi)(functools.partial(attend, qi))


def _flash(q, k, v, batch, seq, tq):
    q3 = q.reshape(batch, seq, MLA_HEADS * MLA_PAD)
    k3 = k.reshape(batch, seq, MLA_HEADS * MLA_PAD)
    v3 = v.reshape(batch, seq, MLA_HEADS * MLA_VDIM)
    wq = FLASH_HEADS * MLA_PAD
    wv = FLASH_HEADS * MLA_VDIM
    out = pl.pallas_call(
        _flash_kernel,
        grid=(batch, MLA_HEADS // FLASH_HEADS, seq // tq),
        in_specs=[pl.BlockSpec((None, tq, wq), lambda b, h, i: (b, i, h)),
                  pl.BlockSpec((None, seq, wq), lambda b, h, i: (b, 0, h)),
                  pl.BlockSpec((None, seq, wv), lambda b, h, i: (b, 0, h))],
        out_specs=pl.BlockSpec((None, tq, wv), lambda b, h, i: (b, i, h)),
        out_shape=jax.ShapeDtypeStruct((batch, seq, MLA_HEADS * MLA_VDIM), BF16),
        compiler_params=_params(("arbitrary", "arbitrary", "arbitrary")),
    )(q3, k3, v3)
    return out.reshape(batch * seq, MLA_HEADS * MLA_VDIM)


def _mla_layouts(w_in, w_uq, q_norm_g, k_norm_g):
    half = MLA_ROPE // 2
    c0 = MLA_Q_LORA + MLA_KV_LORA
    kpe = w_in[:, c0:c0 + MLA_ROPE]
    win = jnp.concatenate([w_in[:, :c0], kpe, kpe, w_in[:, c0 + MLA_ROPE:]], axis=1)
    wq = w_uq.reshape(MLA_Q_LORA, MLA_HEADS, MLA_QK)
    x1 = wq[:, :, MLA_NOPE:MLA_NOPE + half]
    x2 = wq[:, :, MLA_NOPE + half:]
    wuq = jnp.concatenate([wq[:, :, :MLA_NOPE], x1, x2, x2, x1], axis=2).reshape(MLA_Q_LORA, MLA_HEADS * MLA_PAD)
    g1, g2 = q_norm_g[MLA_NOPE:MLA_NOPE + half], q_norm_g[MLA_NOPE + half:]
    qgn = q_norm_g[:MLA_NOPE][None, :]
    qgr = jnp.concatenate([g1, g2, -g2, g1])[None, :]
    kgn = k_norm_g[:MLA_NOPE][None, :]
    kgr = jnp.concatenate([k_norm_g[MLA_NOPE:], k_norm_g[MLA_NOPE:]])[None, :]
    return win, wuq, (qgn, qgr, kgn, kgr)


def kernel(x, mem, positions, mix_norm_g, ffn_norm_g, mem_norm_g, w_mem_kv, mem_q_norm_g, mem_k_norm_g, w_out, w_up, conv_w, conv_b, w_down, hg_w_in, hg_lb_logits, hg_out_norm_g, mla_w_in, mla_qa_norm_g, mla_kva_norm_g, mla_w_uq, mla_w_ukv, mla_q_norm_g, mla_k_norm_g):
    batch, seq, d = x.shape
    n = batch * seq
    tm_big = min(512, seq)
    x2 = x.reshape(n, d)

    grp = jnp.asarray(np.kron(np.eye(MEM_HEADS), np.full((MEM_HDIM, MEM_HDIM), 1.0 / MEM_HDIM)), BF16)
    sums, tri, lev = _hg_tables()
    hg_tables = (jnp.asarray(sums, BF16), jnp.asarray(tri, BF16), jnp.asarray(lev))

    kmem, vmem = _memkv(mem, mem_norm_g[:, None, :], w_mem_kv.astype(BF16),
                        jnp.tile(mem_k_norm_g, (1, MEM_HEADS))[:, None, :], grp)
    qg = jnp.tile(mem_q_norm_g, (1, MEM_HEADS))[:, None, :]
    w_out_b, w_up_b, w_down_b = w_out.astype(BF16), w_up.astype(BF16), w_down.astype(BF16)

    def finish_layer(xin, y, qm, layer):
        return _layer_tail(xin, y, qm, kmem, vmem, layer, qg, grp, w_out_b, ffn_norm_g[:, None, :],
                           w_up_b, conv_w, conv_b[:, None, :], w_down_b, batch, seq, min(1024, seq))

    qs, lf, kk, v, sg, qm = _hg_inproj(x2, mix_norm_g[0][None, :], hg_w_in[0].astype(BF16),
                                        hg_lb_logits, tm_big)
    y = _hgrn(qs, lf, kk, v, sg, hg_tables, hg_out_norm_g[0][None, :], batch, seq)
    x2 = finish_layer(x2, y, qm, 0)

    win, wuq, tabs = _mla_layouts(mla_w_in[0], mla_w_uq[0], mla_q_norm_g[0], mla_k_norm_g[0])
    q, k, v, qm = _mla_prep(x2, _rope_tables(positions), mix_norm_g[1][None, :], win.astype(BF16),
                            jnp.asarray(_rope_expand(), BF16),
                            mla_qa_norm_g[0][None, :], mla_kva_norm_g[0][None, :], wuq.astype(BF16),
                            mla_w_ukv[0].astype(BF16), tabs, tm_big)
    y = _flash(q, k, v, batch, seq, min(FLASH_TQ, seq))
    x2 = finish_layer(x2, y, qm, 1)
    return x2.reshape(batch, seq, d)
```

```python
import functools

import numpy as np
import jax
import jax.numpy as jnp
from jax import lax
from jax.experimental import pallas as pl
from jax.experimental.pallas import tpu as pltpu

F32 = jnp.float32
BF16 = jnp.bfloat16
EPS = 1e-6

HG_HEADS = 6
HG_KDIM = 128
HG_VDIM = 128
HG_F = HG_HEADS * HG_KDIM
MLA_HEADS = 6
MLA_Q_LORA = 384
MLA_KV_LORA = 256
MLA_NOPE = 128
MLA_ROPE = 64
MLA_QK = MLA_NOPE + MLA_ROPE
MLA_VDIM = 128
MLA_PAD = 256
ROPE_THETA = 10000.0
MEM_HEADS = 4
MEM_HDIM = 64
MEM_W = MEM_HEADS * MEM_HDIM
MIX_W = 768
D_FF = 2816

LANES = 128
SUBLANES = 8
VMEM_LIMIT = 56 * 1024 * 1024

HG_CHUNK = 128
HG_LEVELS = 7
HG_BIG_LEVELS = 4
HG_STEP_CHUNKS = 4
FFN_CHUNK = 256
LOG2E = 1.4426950408889634
FLASH_HEADS = 2
FLASH_TQ = 512
ROPE_ROWS = 512
MLA_SUB = 256


def _dot(a, b):
    return jnp.dot(a, b, preferred_element_type=F32)


def _dot_nt(a, b):
    return lax.dot_general(a, b, (((1,), (1,)), ((), ())), preferred_element_type=F32)


def _dot_tn(a, b):
    return lax.dot_general(a, b, (((0,), (0,)), ((), ())), preferred_element_type=F32)


def _rms(x, g):
    ms = jnp.mean(x * x, axis=-1, keepdims=True)
    return x * lax.rsqrt(ms + EPS) * g


def _sigmoid(x):
    return 1.0 / (1.0 + jnp.exp(-x))


def _split2(x):
    hi = x.astype(BF16)
    lo = (x - hi.astype(F32)).astype(BF16)
    return hi, lo


def _split3(x):
    hi = x.astype(BF16)
    r = x - hi.astype(F32)
    mid = r.astype(BF16)
    lo = (r - mid.astype(F32)).astype(BF16)
    return hi, mid, lo


def _const_spec(shape):
    nd = len(shape)
    return pl.BlockSpec(shape, lambda *_: (0,) * nd, pipeline_mode=pl.Buffered(1))


def _layer_spec(arr, layer):
    nd = arr.ndim - 1
    return pl.BlockSpec((None,) + arr.shape[1:], lambda *_: (layer,) + (0,) * nd,
                        pipeline_mode=pl.Buffered(1))


def _params(sem):
    return pltpu.CompilerParams(dimension_semantics=sem, vmem_limit_bytes=VMEM_LIMIT)


def _hg_inproj_kernel(x_ref, g_ref, w_ref, lbl_ref, qs_ref, lf_ref, kk_ref, v_ref, sg_ref, qm_ref):
    h = _rms(x_ref[...], g_ref[...]).astype(BF16)
    l = lbl_ref[...]
    e = jnp.exp(l - jnp.max(l, axis=0, keepdims=True))
    lb = e[0:1, :] / jnp.sum(e, axis=0, keepdims=True)
    q = _dot(h, w_ref[:, 0:HG_F])
    qs_ref[...] = q * _sigmoid(q) * (HG_KDIM ** -0.5)
    f = _dot(h, w_ref[:, HG_F:2 * HG_F])
    fg = lb + (1.0 - lb) * _sigmoid(f)
    lf_ref[...] = jnp.log2(fg)
    kk_ref[...] = 1.0 - fg
    v_ref[...] = _dot(h, w_ref[:, 2 * HG_F:3 * HG_F]).astype(BF16)
    gate = _dot(h, w_ref[:, 3 * HG_F:4 * HG_F])
    sg_ref[...] = gate * _sigmoid(gate)
    qm_ref[...] = _dot(h, w_ref[:, 4 * HG_F:4 * HG_F + MEM_W])


def _hg_inproj(x2, g, w, lb_logits, tm):
    n, d = x2.shape
    row = lambda i: (i, 0)
    out768 = pl.BlockSpec((tm, HG_F), row)
    return pl.pallas_call(
        _hg_inproj_kernel,
        grid=(n // tm,),
        in_specs=[pl.BlockSpec((tm, d), row), _const_spec((1, d)), _const_spec(w.shape),
                  _const_spec(lb_logits.shape)],
        out_specs=[out768, out768, out768, out768, out768, pl.BlockSpec((tm, MEM_W), row)],
        out_shape=[jax.ShapeDtypeStruct((n, HG_F), F32), jax.ShapeDtypeStruct((n, HG_F), F32),
                   jax.ShapeDtypeStruct((n, HG_F), F32), jax.ShapeDtypeStruct((n, HG_F), BF16),
                   jax.ShapeDtypeStruct((n, HG_F), F32), jax.ShapeDtypeStruct((n, MEM_W), F32)],
        compiler_params=_params(("arbitrary",)),
    )(x2, g, w, lb_logits)


def _memkv_kernel(mem_ref, g_ref, w_ref, kg_ref, grp_ref, k_ref, v_ref):
    mn = _rms(mem_ref[...], g_ref[...]).astype(BF16)
    kv = _dot(mn, w_ref[...])
    k = kv[:, :MEM_W]
    hi, lo = _split2(k * k)
    ms = _dot(hi, grp_ref[...]) + _dot(lo, grp_ref[...])
    k_ref[...] = (k * lax.rsqrt(ms + EPS) * kg_ref[...]).astype(BF16)
    v_ref[...] = kv[:, MEM_W:].astype(BF16)


def _memkv(mem, g, w, kg, grp):
    depth = g.shape[0]
    b, m, d = mem.shape
    spec_kv = pl.BlockSpec((None, None, m, MEM_W), lambda l, i: (l, i, 0, 0))
    return pl.pallas_call(
        _memkv_kernel,
        grid=(depth, b),
        in_specs=[pl.BlockSpec((None, m, d), lambda l, i: (i, 0, 0)),
                  pl.BlockSpec((None, 1, d), lambda l, i: (l, 0, 0)),
                  pl.BlockSpec((None, d, 2 * MEM_W), lambda l, i: (l, 0, 0)),
                  pl.BlockSpec((None, 1, MEM_W), lambda l, i: (l, 0, 0)),
                  _const_spec(grp.shape)],
        out_specs=[spec_kv, spec_kv],
        out_shape=[jax.ShapeDtypeStruct((depth, b, m, MEM_W), BF16)] * 2,
        compiler_params=_params(("arbitrary", "arbitrary")),
    )(mem, g, w, kg, grp)


def _hg_tables():
    C = HG_CHUNK
    t = np.arange(C)[:, None]
    u = np.arange(C)[None, :]
    small = []
    for l in range(HG_BIG_LEVELS, HG_LEVELS):
        c = C >> (l + 1)
        mid = (t // (2 * c)) * (2 * c) + c
        small.append(np.where(t >= mid, (u >= mid) & (u <= t), (u > t) & (u < mid)))
    tri = (u <= t).astype(np.float32)
    sums = np.concatenate([tri] + small, axis=0).astype(np.float32)
    sums = np.concatenate([sums, sums], axis=1)
    p = np.floor(np.log2(np.maximum(t ^ u, 1))).astype(np.int32)
    lev = np.where(t > u, HG_LEVELS - 1 - p, np.where(t == u, HG_LEVELS, -1)).astype(np.int32)
    return sums, tri, lev


def _hgrn_kernel(qs_ref, lf_ref, kk_ref, v_ref, sg_ref, sums_ref, tri_ref, lev_ref, og_ref, y_ref, st_ref):
    C = HG_CHUNK
    heads = range(HG_HEADS)
    cols = [slice(h * HG_KDIM, (h + 1) * HG_KDIM) for h in heads]
    chunks = [slice(s * C, (s + 1) * C) for s in range(qs_ref.shape[0] // C)]

    @pl.when(pl.program_id(1) == 0)
    def _():
        st_ref[...] = jnp.zeros_like(st_ref)

    lev = lev_ref[...]
    masks = [lev == l for l in range(HG_LEVELS + 1)]

    pairs, inter, diag = {}, {}, {}
    for s, rows in enumerate(chunks):
        hi, mid, lo = _split3(lf_ref[rows, :])
        d_all = _dot(sums_ref[...], jnp.concatenate([hi, mid], axis=0))
        b = d_all[0:C, :] + _dot(tri_ref[...], lo)
        d_small = d_all[C:, :]
        for h in heads:
            q = qs_ref[rows, cols[h]]
            k = kk_ref[rows, cols[h]]
            diag[s, h] = jnp.sum(q * k, axis=-1, keepdims=True)
            qb = q.astype(BF16)
            kb = k.astype(BF16)
            bh = b[:, cols[h]]
            ops = []
            for l in range(HG_LEVELS):
                if l < HG_BIG_LEVELS:
                    c = C >> (l + 1)
                    parts = []
                    for lo_row in range(0, C, 2 * c):
                        r = jnp.broadcast_to(bh[lo_row + c - 1:lo_row + c, :], (c, HG_KDIM))
                        parts += [r - bh[lo_row:lo_row + c, :], bh[lo_row + c:lo_row + 2 * c, :] - r]
                    e = jnp.exp2(jnp.concatenate(parts, axis=0))
                else:
                    e = jnp.exp2(d_small[(l - HG_BIG_LEVELS) * C:(l - HG_BIG_LEVELS + 1) * C, cols[h]])
                eb = e.astype(BF16)
                ops.append((qb * eb, kb * eb))
            pairs[s, h] = ops
            e_pre = jnp.exp2(bh)
            e_suf = jnp.exp2(bh[C - 1:C, :] - bh)
            inter[s, h] = (qb * e_pre.astype(BF16), kb * e_suf.astype(BF16), e_pre[C - 1:C, :])

    scores = {}
    for s in range(len(chunks)):
        for h in heads:
            a = jnp.where(masks[HG_LEVELS], diag[s, h], 0.0)
            for l in range(HG_LEVELS):
                a = jnp.where(masks[l], _dot_nt(*pairs[s, h][l]), a)
            scores[s, h] = a.astype(BF16)

    for s, rows in enumerate(chunks):
        for h in heads:
            qe, ke, dec = inter[s, h]
            v = v_ref[rows, cols[h]]
            st = st_ref[h]
            o = _dot_nt(qe, st.astype(BF16)) + _dot(scores[s, h], v)
            st_ref[h] = st * dec + _dot_tn(v, ke)
            y_ref[rows, cols[h]] = (_rms(o, og_ref[...]) * sg_ref[rows, cols[h]]).astype(BF16)


def _hgrn(qs, lf, kk, v, sg, tables, og, batch, seq):
    n = qs.shape[0]
    step = HG_CHUNK * HG_STEP_CHUNKS
    ns = seq // step
    row = lambda b, c: (b * ns + c, 0)
    spec = pl.BlockSpec((step, HG_F), row)
    return pl.pallas_call(
        _hgrn_kernel,
        grid=(batch, ns),
        in_specs=[spec, spec, spec, spec, spec] + [_const_spec(t.shape) for t in tables]
                 + [_const_spec(og.shape)],
        out_specs=spec,
        out_shape=jax.ShapeDtypeStruct((n, HG_F), BF16),
        scratch_shapes=[pltpu.VMEM((HG_HEADS, HG_VDIM, HG_KDIM), F32)],
        compiler_params=_params(("arbitrary", "arbitrary")),
    )(qs, lf, kk, v, sg, *tables, og)


def _mix_out(x, y_ref, qm_ref, km_ref, vm_ref, qg_ref, grp_ref, w_ref):
    q = qm_ref[...]
    hi, lo = _split2(q * q)
    ms = _dot(hi, grp_ref[...]) + _dot(lo, grp_ref[...])
    qn = (q * lax.rsqrt(ms + EPS) * (qg_ref[...] * (MEM_HDIM ** -0.5 * LOG2E))).astype(BF16)
    tm = qn.shape[0]
    head = lax.broadcasted_iota(jnp.int32, qn.shape, 1) // MEM_HDIM
    qs = jnp.concatenate([jnp.where(head == h, qn, jnp.zeros_like(qn)) for h in range(MEM_HEADS)], axis=0)
    s = _dot_nt(qs, km_ref[...])
    p = jnp.exp2(s - jnp.max(s, axis=-1, keepdims=True))
    o = _dot(p.astype(BF16), vm_ref[...]) / jnp.sum(p, axis=-1, keepdims=True)
    m = o[0:tm, :]
    for h in range(1, MEM_HEADS):
        m = jnp.where(head == h, o[h * tm:(h + 1) * tm, :], m)
    return (x + _dot(y_ref[...], w_ref[0:MIX_W, :])
            + _dot(m.astype(BF16), w_ref[MIX_W:MIX_W + MEM_W, :]))


def _shift_row(t, prev):
    r = pltpu.roll(t, 1, axis=0)
    pr = pltpu.roll(prev, 1, axis=0)
    sub = lax.broadcasted_iota(jnp.int32, pr.shape, 0)
    head = jnp.where(sub < 1, pr, r[0:SUBLANES, :])
    return jnp.concatenate([head, r[SUBLANES:, :]], axis=0)


def _conv_ffn(x, g_ref, wup_ref, cw_ref, cb_ref, wdn_ref, carry_ref, act_ref):
    tm = x.shape[0]
    h = _rms(x, g_ref[...]).astype(BF16)

    def conv_slice(c0):
        cols = slice(c0, c0 + FFN_CHUNK)
        u = _dot(h, wup_ref[:, cols])
        prev = carry_ref[:, cols]
        carry_ref[:, cols] = u[tm - SUBLANES:, :]
        w0, w1, w2 = cw_ref[0:1, cols], cw_ref[1:2, cols], cw_ref[2:3, cols]
        t0, p0 = w0 * u, w0 * prev
        t1 = _shift_row(t0, p0) + w1 * u
        p1 = pltpu.roll(p0, 1, axis=0) + w1 * prev
        return _shift_row(t1, p1) + (w2 * u + cb_ref[:, cols])

    for c in range(D_FF // FFN_CHUNK):
        a = conv_slice(c * FFN_CHUNK).astype(BF16)
        b = conv_slice(D_FF + c * FFN_CHUNK).astype(BF16)
        sig = 1.0 / (1.0 + jnp.exp2(a * (-LOG2E)))
        act_ref[:, c * FFN_CHUNK:(c + 1) * FFN_CHUNK] = a * b * sig
    return x + _dot(act_ref[...], wdn_ref[...])


def _tail_kernel(x_ref, y_ref, qm_ref, km_ref, vm_ref, qg_ref, grp_ref, wout_ref,
                 g_ref, wup_ref, cw_ref, cb_ref, wdn_ref, o_ref, carry_ref, act_ref):
    @pl.when(pl.program_id(1) == 0)
    def _():
        carry_ref[...] = jnp.zeros_like(carry_ref)

    x1 = _mix_out(x_ref[...], y_ref, qm_ref, km_ref, vm_ref, qg_ref, grp_ref, wout_ref)
    o_ref[...] = _conv_ffn(x1, g_ref, wup_ref, cw_ref, cb_ref, wdn_ref, carry_ref, act_ref)


def _layer_tail(x2, y, qm, kmem, vmem, layer, qg, grp, wout, g, wup, cw, cb, wdn, batch, seq, tm):
    n, d = x2.shape
    nt = seq // tm
    row = lambda b, t: (b * nt + t, 0)
    mem_spec = pl.BlockSpec((None, None) + kmem.shape[2:], lambda b, t: (layer, b, 0, 0))
    return pl.pallas_call(
        _tail_kernel,
        grid=(batch, nt),
        in_specs=[pl.BlockSpec((tm, d), row), pl.BlockSpec((tm, MIX_W), row),
                  pl.BlockSpec((tm, MEM_W), row), mem_spec, mem_spec,
                  _layer_spec(qg, layer), _const_spec(grp.shape), _layer_spec(wout, layer)]
                 + [_layer_spec(p, layer) for p in (g, wup, cw, cb, wdn)],
        out_specs=pl.BlockSpec((tm, d), row),
        out_shape=jax.ShapeDtypeStruct((n, d), F32),
        scratch_shapes=[pltpu.VMEM((SUBLANES, 2 * D_FF), F32), pltpu.VMEM((tm, D_FF), BF16)],
        compiler_params=_params(("arbitrary", "arbitrary")),
    )(x2, y, qm, kmem, vmem, qg, grp, wout, g, wup, cw, cb, wdn)


def _rope_kernel(pos_ref, inv_ref, cos_ref, sin_ref):
    ang = pos_ref[...].astype(F32) * inv_ref[...]
    cos_ref[...] = jnp.cos(ang)
    sin_ref[...] = jnp.sin(ang)


def _rope_tables(positions):
    half = MLA_ROPE // 2
    per_row = LANES // half
    n = positions.size
    rows = n // per_row
    tr = min(ROPE_ROWS, rows)
    inv = ROPE_THETA ** (-jnp.arange(half, dtype=F32) / half)
    pos = jnp.broadcast_to(positions.reshape(n, 1), (n, half)).reshape(rows, LANES)
    spec = pl.BlockSpec((tr, LANES), lambda i: (i, 0))
    cos, sin = pl.pallas_call(
        _rope_kernel,
        grid=(rows // tr,),
        in_specs=[spec, _const_spec((1, LANES))],
        out_specs=[spec, spec],
        out_shape=[jax.ShapeDtypeStruct((rows, LANES), F32)] * 2,
        compiler_params=_params(("arbitrary",)),
    )(pos, jnp.tile(inv, per_row)[None, :])
    return jnp.concatenate([cos.reshape(n, half), sin.reshape(n, half)], axis=1)


def _rope_expand():
    half = MLA_ROPE // 2
    e = np.zeros((2 * half, 3 * LANES), np.float32)
    j = np.arange(half)
    for blk in range(2):
        e[j, blk * half + j] = 1.0
        e[half + j, (2 + blk) * half + j] = 1.0
    for blk in range(4):
        e[j, LANES + blk * half + j] = 1.0
        e[half + j, 2 * LANES + blk * half + j] = -1.0 if blk % 2 == 0 else 1.0
    return e


def _mla_prep_kernel(x_ref, cs_ref, g_ref, win_ref, exp_ref, qag_ref, kvag_ref, wuq_ref, wukv_ref,
                     qgn_ref, qgr_ref, kgn_ref, kgr_ref,
                     q_ref, k_ref, v_ref, qm_ref):
    subs = [slice(r * MLA_SUB, (r + 1) * MLA_SUB) for r in range(x_ref.shape[0] // MLA_SUB)]
    c0 = MLA_Q_LORA + MLA_KV_LORA
    zs = [_dot(_rms(x_ref[rows, :], g_ref[...]).astype(BF16), win_ref[...]) for rows in subs]
    qfs = [_dot(_rms(z[:, 0:MLA_Q_LORA], qag_ref[...]).astype(BF16), wuq_ref[...]) for z in zs]
    kvfs = [_dot(_rms(z[:, MLA_Q_LORA:c0], kvag_ref[...]).astype(BF16), wukv_ref[...]) for z in zs]

    for rows, z, qf, kvf in zip(subs, zs, qfs, kvfs):
        kpe2 = z[:, c0:c0 + LANES]
        qm_ref[rows, :] = z[:, c0 + LANES:]

        hi, lo = _split2(cs_ref[rows, :])
        rope = _dot(hi, exp_ref[...]) + _dot(lo, exp_ref[...])
        tq = rope[:, 0:LANES] * qgr_ref[...]
        cos = rope[:, LANES:2 * LANES]
        sin_signed = rope[:, 2 * LANES:3 * LANES]

        for hd in range(MLA_HEADS):
            qn = qf[:, hd * MLA_PAD:hd * MLA_PAD + MLA_NOPE]
            qr = qf[:, hd * MLA_PAD + MLA_NOPE:(hd + 1) * MLA_PAD]
            ss = jnp.sum(qn * qn, axis=-1, keepdims=True) + 0.5 * jnp.sum(qr * qr, axis=-1, keepdims=True)
            rs = lax.rsqrt(ss * (1.0 / MLA_QK) + EPS) * (MLA_QK ** -0.5 * LOG2E)
            q_ref[rows, hd * MLA_PAD:hd * MLA_PAD + MLA_NOPE] = (qn * rs * qgn_ref[...]).astype(BF16)
            q_ref[rows, hd * MLA_PAD + MLA_NOPE:(hd + 1) * MLA_PAD] = (qr * rs * tq).astype(BF16)

        pk = kpe2 * kgr_ref[...]
        kr = pk * cos + pltpu.roll(pk, MLA_ROPE // 2, axis=1) * sin_signed
        sk = 0.5 * jnp.sum(kpe2 * kpe2, axis=-1, keepdims=True)
        for hd in range(MLA_HEADS):
            kn = kvf[:, hd * MLA_PAD:hd * MLA_PAD + MLA_NOPE]
            ss = jnp.sum(kn * kn, axis=-1, keepdims=True) + sk
            rs = lax.rsqrt(ss * (1.0 / MLA_QK) + EPS)
            k_ref[rows, hd * MLA_PAD:hd * MLA_PAD + MLA_NOPE] = (kn * rs * kgn_ref[...]).astype(BF16)
            k_ref[rows, hd * MLA_PAD + MLA_NOPE:(hd + 1) * MLA_PAD] = (kr * rs).astype(BF16)
            v_ref[rows, hd * MLA_PAD:hd * MLA_PAD + MLA_VDIM] = kvf[:, hd * MLA_PAD + MLA_NOPE:(hd + 1) * MLA_PAD].astype(BF16)
            v_ref[rows, hd * MLA_PAD + MLA_VDIM:(hd + 1) * MLA_PAD] = jnp.ones((MLA_SUB, MLA_PAD - MLA_VDIM), BF16)


def _mla_prep(x2, cs, g, win, expand, qag, kvag, wuq, wukv, tabs, tm):
    n, d = x2.shape
    row = lambda i: (i, 0)
    consts = [g, win, expand, qag, kvag, wuq, wukv] + list(tabs)
    return pl.pallas_call(
        _mla_prep_kernel,
        grid=(n // tm,),
        in_specs=[pl.BlockSpec((tm, d), row), pl.BlockSpec((tm, MLA_ROPE), row)]
                 + [_const_spec(c.shape) for c in consts],
        out_specs=[pl.BlockSpec((tm, MLA_HEADS * MLA_PAD), row), pl.BlockSpec((tm, MLA_HEADS * MLA_PAD), row),
                   pl.BlockSpec((tm, MLA_HEADS * MLA_PAD), row), pl.BlockSpec((tm, MEM_W), row)],
        out_shape=[jax.ShapeDtypeStruct((n, MLA_HEADS * MLA_PAD), BF16),
                   jax.ShapeDtypeStruct((n, MLA_HEADS * MLA_PAD), BF16),
                   jax.ShapeDtypeStruct((n, MLA_HEADS * MLA_PAD), BF16),
                   jax.ShapeDtypeStruct((n, MEM_W), F32)],
        compiler_params=_params(("arbitrary",)),
    )(x2, cs, *consts)


def _flash_kernel(qa_ref, qb_ref, k_ref, v_ref, o_ref):
    tq = qa_ref.shape[0]
    n_tiles = k_ref.shape[0] // tq
    r = lax.broadcasted_iota(jnp.int32, (tq, tq), 0)
    c = lax.broadcasted_iota(jnp.int32, (tq, tq), 1)
    heads = range(FLASH_HEADS)
    kc = [slice(h * MLA_PAD, (h + 1) * MLA_PAD) for h in heads]
    vc = [slice(h * MLA_VDIM, (h + 1) * MLA_VDIM) for h in heads]

    def attend(tiles):
        chains = [(q_ref, qi, h) for q_ref, qi in tiles for h in heads]
        sd = [_dot_nt(q_ref[:, kc[h]], k_ref[qi * tq:(qi + 1) * tq, kc[h]]) for q_ref, qi, h in chains]
        sp = [_dot_nt(q_ref[:, kc[h]], k_ref[0:qi * tq, kc[h]]) if qi > 0 else None for q_ref, qi, h in chains]
        pd, pp = [], []
        for n, (q_ref, qi, h) in enumerate(chains):
            s = jnp.where(c <= r, sd[n], -jnp.inf)
            m = jnp.max(s, axis=-1, keepdims=True)
            if qi > 0:
                m = jnp.maximum(m, jnp.max(sp[n], axis=-1, keepdims=True))
                pp.append(jnp.exp2((sp[n] - m).astype(BF16)))
            else:
                pp.append(None)
            pd.append(jnp.exp2((s - m).astype(BF16)))
        for n, (q_ref, qi, h) in enumerate(chains):
            acc = _dot(pd[n], v_ref[qi * tq:(qi + 1) * tq, kc[h]])
            if qi > 0:
                acc = acc + _dot(pp[n], v_ref[0:qi * tq, kc[h]])
            o_ref[qi * tq:(qi + 1) * tq, vc[h]] = (acc[:, 0:MLA_VDIM] / acc[:, MLA_VDIM:]).astype(BF16)

    for step in range((n_tiles + 1) // 2):
        late = n_tiles - 1 - step
        tiles = [(qa_ref, step)] + ([(qb_ref, late)] if late != step else [])
        pl.when(pl.program_id(2) == step)(functools.partial(attend, tiles))


def _flash(q, k, v, batch, seq, tq):
    q3 = q.reshape(batch, seq, MLA_HEADS * MLA_PAD)
    k3 = k.reshape(batch, seq, MLA_HEADS * MLA_PAD)
    v3 = v.reshape(batch, seq, MLA_HEADS * MLA_PAD)
    wq = FLASH_HEADS * MLA_PAD
    wv = FLASH_HEADS * MLA_VDIM
    n_tiles = seq // tq
    out = pl.pallas_call(
        _flash_kernel,
        grid=(batch, MLA_HEADS // FLASH_HEADS, (n_tiles + 1) // 2),
        in_specs=[pl.BlockSpec((None, tq, wq), lambda b, h, i: (b, i, h)),
                  pl.BlockSpec((None, tq, wq), lambda b, h, i: (b, n_tiles - 1 - i, h)),
                  pl.BlockSpec((None, seq, wq), lambda b, h, i: (b, 0, h)),
                  pl.BlockSpec((None, seq, wq), lambda b, h, i: (b, 0, h))],
        out_specs=pl.BlockSpec((None, seq, wv), lambda b, h, i: (b, 0, h)),
        out_shape=jax.ShapeDtypeStruct((batch, seq, MLA_HEADS * MLA_VDIM), BF16),
        compiler_params=_params(("arbitrary", "arbitrary", "arbitrary")),
    )(q3, q3, k3, v3)
    return out.reshape(batch * seq, MLA_HEADS * MLA_VDIM)


def _mla_layouts(w_in, w_uq, q_norm_g, k_norm_g):
    half = MLA_ROPE // 2
    c0 = MLA_Q_LORA + MLA_KV_LORA
    kpe = w_in[:, c0:c0 + MLA_ROPE]
    win = jnp.concatenate([w_in[:, :c0], kpe, kpe, w_in[:, c0 + MLA_ROPE:]], axis=1)
    wq = w_uq.reshape(MLA_Q_LORA, MLA_HEADS, MLA_QK)
    x1 = wq[:, :, MLA_NOPE:MLA_NOPE + half]
    x2 = wq[:, :, MLA_NOPE + half:]
    wuq = jnp.concatenate([wq[:, :, :MLA_NOPE], x1, x2, x2, x1], axis=2).reshape(MLA_Q_LORA, MLA_HEADS * MLA_PAD)
    g1, g2 = q_norm_g[MLA_NOPE:MLA_NOPE + half], q_norm_g[MLA_NOPE + half:]
    qgn = q_norm_g[:MLA_NOPE][None, :]
    qgr = jnp.concatenate([g1, g2, -g2, g1])[None, :]
    kgn = k_norm_g[:MLA_NOPE][None, :]
    kgr = jnp.concatenate([k_norm_g[MLA_NOPE:], k_norm_g[MLA_NOPE:]])[None, :]
    return win, wuq, (qgn, qgr, kgn, kgr)


def kernel(x, mem, positions, mix_norm_g, ffn_norm_g, mem_norm_g, w_mem_kv, mem_q_norm_g, mem_k_norm_g, w_out, w_up, conv_w, conv_b, w_down, hg_w_in, hg_lb_logits, hg_out_norm_g, mla_w_in, mla_qa_norm_g, mla_kva_norm_g, mla_w_uq, mla_w_ukv, mla_q_norm_g, mla_k_norm_g):
    batch, seq, d = x.shape
    n = batch * seq
    tm_big = min(512, seq)
    x2 = x.reshape(n, d)

    grp = jnp.asarray(np.kron(np.eye(MEM_HEADS), np.full((MEM_HDIM, MEM_HDIM), 1.0 / MEM_HDIM)), BF16)
    sums, tri, lev = _hg_tables()
    hg_tables = (jnp.asarray(sums, BF16), jnp.asarray(tri, BF16), jnp.asarray(lev))

    kmem, vmem = _memkv(mem, mem_norm_g[:, None, :], w_mem_kv.astype(BF16),
                        jnp.tile(mem_k_norm_g, (1, MEM_HEADS))[:, None, :], grp)
    qg = jnp.tile(mem_q_norm_g, (1, MEM_HEADS))[:, None, :]
    w_out_b, w_up_b, w_down_b = w_out.astype(BF16), w_up.astype(BF16), w_down.astype(BF16)

    def finish_layer(xin, y, qm, layer):
        return _layer_tail(xin, y, qm, kmem, vmem, layer, qg, grp, w_out_b, ffn_norm_g[:, None, :],
                           w_up_b, conv_w, conv_b[:, None, :], w_down_b, batch, seq, min(1024, seq))

    qs, lf, kk, v, sg, qm = _hg_inproj(x2, mix_norm_g[0][None, :], hg_w_in[0].astype(BF16),
                                        hg_lb_logits, tm_big)
    y = _hgrn(qs, lf, kk, v, sg, hg_tables, hg_out_norm_g[0][None, :], batch, seq)
    x2 = finish_layer(x2, y, qm, 0)

    win, wuq, tabs = _mla_layouts(mla_w_in[0], mla_w_uq[0], mla_q_norm_g[0], mla_k_norm_g[0])
    q, k, v, qm = _mla_prep(x2, _rope_tables(positions), mix_norm_g[1][None, :], win.astype(BF16),
                            jnp.asarray(_rope_expand(), BF16),
                            mla_qa_norm_g[0][None, :], mla_kva_norm_g[0][None, :], wuq.astype(BF16),
                            mla_w_ukv[0].astype(BF16), tabs, tm_big)
    y = _flash(q, k, v, batch, seq, min(FLASH_TQ, seq))
    x2 = finish_layer(x2, y, qm, 1)
    return x2.reshape(batch, seq, d)
```

```python
import functools

import numpy as np
import jax
import jax.numpy as jnp
from jax import lax
from jax.experimental import pallas as pl
from jax.experimental.pallas import tpu as pltpu

F32 = jnp.float32
BF16 = jnp.bfloat16
EPS = 1e-6

HG_HEADS = 6
HG_KDIM = 128
HG_VDIM = 128
HG_F = HG_HEADS * HG_KDIM
MLA_HEADS = 6
MLA_Q_LORA = 384
MLA_KV_LORA = 256
MLA_NOPE = 128
MLA_ROPE = 64
MLA_QK = MLA_NOPE + MLA_ROPE
MLA_VDIM = 128
MLA_PAD = 256
ROPE_THETA = 10000.0
MEM_HEADS = 4
MEM_HDIM = 64
MEM_W = MEM_HEADS * MEM_HDIM
MIX_W = 768
D_FF = 2816

LANES = 128
SUBLANES = 8
VMEM_LIMIT = 56 * 1024 * 1024

HG_CHUNK = 128
HG_LEVELS = 7
HG_BIG_LEVELS = 4
HG_STEP_CHUNKS = 4
FFN_CHUNK = 256
LOG2E = 1.4426950408889634
FLASH_HEADS = 2
FLASH_TQ = 512
ROPE_ROWS = 512
MLA_SUB = 256


def _dot(a, b):
    return jnp.dot(a, b, preferred_element_type=F32)


def _dot_nt(a, b):
    return lax.dot_general(a, b, (((1,), (1,)), ((), ())), preferred_element_type=F32)


def _dot_tn(a, b):
    return lax.dot_general(a, b, (((0,), (0,)), ((), ())), preferred_element_type=F32)


def _rms(x, g):
    ms = jnp.mean(x * x, axis=-1, keepdims=True)
    return x * lax.rsqrt(ms + EPS) * g


def _sigmoid(x):
    return 1.0 / (1.0 + jnp.exp(-x))


def _split2(x):
    hi = x.astype(BF16)
    lo = (x - hi.astype(F32)).astype(BF16)
    return hi, lo


def _split3(x):
    hi = x.astype(BF16)
    r = x - hi.astype(F32)
    mid = r.astype(BF16)
    lo = (r - mid.astype(F32)).astype(BF16)
    return hi, mid, lo


def _const_spec(shape):
    nd = len(shape)
    return pl.BlockSpec(shape, lambda *_: (0,) * nd, pipeline_mode=pl.Buffered(1))


def _layer_spec(arr, layer):
    nd = arr.ndim - 1
    return pl.BlockSpec((None,) + arr.shape[1:], lambda *_: (layer,) + (0,) * nd,
                        pipeline_mode=pl.Buffered(1))


def _params(sem):
    return pltpu.CompilerParams(dimension_semantics=sem, vmem_limit_bytes=VMEM_LIMIT)


def _hg_inproj_kernel(x_ref, g_ref, w_ref, lbl_ref, qs_ref, lf_ref, kk_ref, v_ref, sg_ref, qm_ref):
    h = _rms(x_ref[...], g_ref[...]).astype(BF16)
    l = lbl_ref[...]
    e = jnp.exp(l - jnp.max(l, axis=0, keepdims=True))
    lb = e[0:1, :] / jnp.sum(e, axis=0, keepdims=True)
    q = _dot(h, w_ref[:, 0:HG_F])
    qs_ref[...] = (q * _sigmoid(q) * (HG_KDIM ** -0.5)).astype(BF16)
    f = _dot(h, w_ref[:, HG_F:2 * HG_F])
    fg = lb + (1.0 - lb) * _sigmoid(f)
    lf_ref[...] = jnp.log2(fg)
    kk_ref[...] = (1.0 - fg).astype(BF16)
    v_ref[...] = _dot(h, w_ref[:, 2 * HG_F:3 * HG_F]).astype(BF16)
    gate = _dot(h, w_ref[:, 3 * HG_F:4 * HG_F])
    sg_ref[...] = (gate * _sigmoid(gate)).astype(BF16)
    qm_ref[...] = _dot(h, w_ref[:, 4 * HG_F:4 * HG_F + MEM_W])


def _hg_inproj(x2, g, w, lb_logits, tm):
    n, d = x2.shape
    row = lambda i: (i, 0)
    out768 = pl.BlockSpec((tm, HG_F), row)
    return pl.pallas_call(
        _hg_inproj_kernel,
        grid=(n // tm,),
        in_specs=[pl.BlockSpec((tm, d), row), _const_spec((1, d)), _const_spec(w.shape),
                  _const_spec(lb_logits.shape)],
        out_specs=[out768, out768, out768, out768, out768, pl.BlockSpec((tm, MEM_W), row)],
        out_shape=[jax.ShapeDtypeStruct((n, HG_F), BF16), jax.ShapeDtypeStruct((n, HG_F), F32),
                   jax.ShapeDtypeStruct((n, HG_F), BF16), jax.ShapeDtypeStruct((n, HG_F), BF16),
                   jax.ShapeDtypeStruct((n, HG_F), BF16), jax.ShapeDtypeStruct((n, MEM_W), F32)],
        compiler_params=_params(("arbitrary",)),
    )(x2, g, w, lb_logits)


def _memkv_kernel(mem_ref, g_ref, w_ref, kg_ref, grp_ref, k_ref, v_ref):
    mn = _rms(mem_ref[...], g_ref[...]).astype(BF16)
    kv = _dot(mn, w_ref[...])
    k = kv[:, :MEM_W]
    hi, lo = _split2(k * k)
    ms = _dot(hi, grp_ref[...]) + _dot(lo, grp_ref[...])
    k_ref[...] = (k * lax.rsqrt(ms + EPS) * kg_ref[...]).astype(BF16)
    v_ref[...] = kv[:, MEM_W:].astype(BF16)


def _memkv(mem, g, w, kg, grp):
    depth = g.shape[0]
    b, m, d = mem.shape
    spec_kv = pl.BlockSpec((None, None, m, MEM_W), lambda l, i: (l, i, 0, 0))
    return pl.pallas_call(
        _memkv_kernel,
        grid=(depth, b),
        in_specs=[pl.BlockSpec((None, m, d), lambda l, i: (i, 0, 0)),
                  pl.BlockSpec((None, 1, d), lambda l, i: (l, 0, 0)),
                  pl.BlockSpec((None, d, 2 * MEM_W), lambda l, i: (l, 0, 0)),
                  pl.BlockSpec((None, 1, MEM_W), lambda l, i: (l, 0, 0)),
                  _const_spec(grp.shape)],
        out_specs=[spec_kv, spec_kv],
        out_shape=[jax.ShapeDtypeStruct((depth, b, m, MEM_W), BF16)] * 2,
        compiler_params=_params(("arbitrary", "arbitrary")),
    )(mem, g, w, kg, grp)


def _hg_tables():
    C = HG_CHUNK
    t = np.arange(C)[:, None]
    u = np.arange(C)[None, :]
    small = []
    for l in range(HG_BIG_LEVELS, HG_LEVELS):
        c = C >> (l + 1)
        mid = (t // (2 * c)) * (2 * c) + c
        small.append(np.where(t >= mid, (u >= mid) & (u <= t), (u > t) & (u < mid)))
    tri = (u <= t).astype(np.float32)
    sums = np.concatenate([tri] + small, axis=0).astype(np.float32)
    sums = np.concatenate([sums, sums], axis=1)
    p = np.floor(np.log2(np.maximum(t ^ u, 1))).astype(np.int32)
    lev = np.where(t > u, HG_LEVELS - 1 - p, np.where(t == u, HG_LEVELS, -1)).astype(np.int32)
    return sums, tri, lev


def _hgrn_kernel(qs_ref, lf_ref, kk_ref, v_ref, sg_ref, sums_ref, tri_ref, lev_ref, og_ref, y_ref, st_ref):
    C = HG_CHUNK
    heads = range(HG_HEADS)
    cols = [slice(h * HG_KDIM, (h + 1) * HG_KDIM) for h in heads]
    chunks = [slice(s * C, (s + 1) * C) for s in range(qs_ref.shape[0] // C)]

    @pl.when(pl.program_id(1) == 0)
    def _():
        st_ref[...] = jnp.zeros_like(st_ref)

    lev = lev_ref[...]
    masks = [lev == l for l in range(HG_LEVELS + 1)]

    pairs, inter, diag = {}, {}, {}
    for s, rows in enumerate(chunks):
        hi, mid, lo = _split3(lf_ref[rows, :])
        d_all = _dot(sums_ref[...], jnp.concatenate([hi, mid], axis=0))
        b = d_all[0:C, :] + _dot(tri_ref[...], lo)
        d_small = d_all[C:, :]
        for h in heads:
            qb = qs_ref[rows, cols[h]]
            kb = kk_ref[rows, cols[h]]
            diag[s, h] = jnp.sum(qb.astype(F32) * kb.astype(F32), axis=-1, keepdims=True)
            bh = b[:, cols[h]]
            ops = []
            for l in range(HG_LEVELS):
                if l < HG_BIG_LEVELS:
                    c = C >> (l + 1)
                    parts = []
                    for lo_row in range(0, C, 2 * c):
                        r = jnp.broadcast_to(bh[lo_row + c - 1:lo_row + c, :], (c, HG_KDIM))
                        parts += [r - bh[lo_row:lo_row + c, :], bh[lo_row + c:lo_row + 2 * c, :] - r]
                    e = jnp.exp2(jnp.concatenate(parts, axis=0))
                else:
                    e = jnp.exp2(d_small[(l - HG_BIG_LEVELS) * C:(l - HG_BIG_LEVELS + 1) * C, cols[h]])
                eb = e.astype(BF16)
                ops.append((qb * eb, kb * eb))
            pairs[s, h] = ops
            e_pre = jnp.exp2(bh)
            e_suf = jnp.exp2(bh[C - 1:C, :] - bh)
            inter[s, h] = (qb * e_pre.astype(BF16), kb * e_suf.astype(BF16), e_pre[C - 1:C, :])

    scores = {}
    for s in range(len(chunks)):
        for h in heads:
            a = jnp.where(masks[HG_LEVELS], diag[s, h], 0.0)
            for l in range(HG_LEVELS):
                a = jnp.where(masks[l], _dot_nt(*pairs[s, h][l]), a)
            scores[s, h] = a.astype(BF16)

    for s, rows in enumerate(chunks):
        for h in heads:
            qe, ke, dec = inter[s, h]
            v = v_ref[rows, cols[h]]
            st = st_ref[h]
            o = _dot_nt(qe, st.astype(BF16)) + _dot(scores[s, h], v)
            st_ref[h] = st * dec + _dot_tn(v, ke)
            y_ref[rows, cols[h]] = (_rms(o, og_ref[...]) * sg_ref[rows, cols[h]]).astype(BF16)


def _hgrn(qs, lf, kk, v, sg, tables, og, batch, seq):
    n = qs.shape[0]
    step = HG_CHUNK * HG_STEP_CHUNKS
    ns = seq // step
    row = lambda b, c: (b * ns + c, 0)
    spec = pl.BlockSpec((step, HG_F), row)
    return pl.pallas_call(
        _hgrn_kernel,
        grid=(batch, ns),
        in_specs=[spec, spec, spec, spec, spec] + [_const_spec(t.shape) for t in tables]
                 + [_const_spec(og.shape)],
        out_specs=spec,
        out_shape=jax.ShapeDtypeStruct((n, HG_F), BF16),
        scratch_shapes=[pltpu.VMEM((HG_HEADS, HG_VDIM, HG_KDIM), F32)],
        compiler_params=_params(("arbitrary", "arbitrary")),
    )(qs, lf, kk, v, sg, *tables, og)


def _mix_out(x, y_ref, qm_ref, km_ref, vm_ref, qg_ref, grp_ref, w_ref):
    q = qm_ref[...]
    hi, lo = _split2(q * q)
    ms = _dot(hi, grp_ref[...]) + _dot(lo, grp_ref[...])
    qn = (q * lax.rsqrt(ms + EPS) * (qg_ref[...] * (MEM_HDIM ** -0.5 * LOG2E))).astype(BF16)
    tm = qn.shape[0]
    head = lax.broadcasted_iota(jnp.int32, qn.shape, 1) // MEM_HDIM
    qs = jnp.concatenate([jnp.where(head == h, qn, jnp.zeros_like(qn)) for h in range(MEM_HEADS)], axis=0)
    s = _dot_nt(qs, km_ref[...])
    p = jnp.exp2(s - jnp.max(s, axis=-1, keepdims=True))
    o = _dot(p.astype(BF16), vm_ref[...]) / jnp.sum(p, axis=-1, keepdims=True)
    m = o[0:tm, :]
    for h in range(1, MEM_HEADS):
        m = jnp.where(head == h, o[h * tm:(h + 1) * tm, :], m)
    return (x + _dot(y_ref[...], w_ref[0:MIX_W, :])
            + _dot(m.astype(BF16), w_ref[MIX_W:MIX_W + MEM_W, :]))


def _shift_row(t, prev):
    r = pltpu.roll(t, 1, axis=0)
    pr = pltpu.roll(prev, 1, axis=0)
    sub = lax.broadcasted_iota(jnp.int32, pr.shape, 0)
    head = jnp.where(sub < 1, pr, r[0:SUBLANES, :])
    return jnp.concatenate([head, r[SUBLANES:, :]], axis=0)


def _conv_ffn(x, g_ref, wup_ref, cw_ref, cb_ref, wdn_ref, carry_ref, act_ref):
    tm = x.shape[0]
    h = _rms(x, g_ref[...]).astype(BF16)

    def conv_slice(c0):
        cols = slice(c0, c0 + FFN_CHUNK)
        u = _dot(h, wup_ref[:, cols])
        prev = carry_ref[:, cols]
        carry_ref[:, cols] = u[tm - SUBLANES:, :]
        w0, w1, w2 = cw_ref[0:1, cols], cw_ref[1:2, cols], cw_ref[2:3, cols]
        t0, p0 = w0 * u, w0 * prev
        t1 = _shift_row(t0, p0) + w1 * u
        p1 = pltpu.roll(p0, 1, axis=0) + w1 * prev
        return _shift_row(t1, p1) + (w2 * u + cb_ref[:, cols])

    for c in range(D_FF // FFN_CHUNK):
        a = conv_slice(c * FFN_CHUNK).astype(BF16)
        b = conv_slice(D_FF + c * FFN_CHUNK).astype(BF16)
        sig = 1.0 / (1.0 + jnp.exp2(a * (-LOG2E)))
        act_ref[:, c * FFN_CHUNK:(c + 1) * FFN_CHUNK] = a * b * sig
    return x + _dot(act_ref[...], wdn_ref[...])


def _tail_kernel(x_ref, y_ref, qm_ref, km_ref, vm_ref, qg_ref, grp_ref, wout_ref,
                 g_ref, wup_ref, cw_ref, cb_ref, wdn_ref, o_ref, carry_ref, act_ref):
    @pl.when(pl.program_id(1) == 0)
    def _():
        carry_ref[...] = jnp.zeros_like(carry_ref)

    x1 = _mix_out(x_ref[...], y_ref, qm_ref, km_ref, vm_ref, qg_ref, grp_ref, wout_ref)
    o_ref[...] = _conv_ffn(x1, g_ref, wup_ref, cw_ref, cb_ref, wdn_ref, carry_ref, act_ref)


def _layer_tail(x2, y, qm, kmem, vmem, layer, qg, grp, wout, g, wup, cw, cb, wdn, batch, seq, tm):
    n, d = x2.shape
    nt = seq // tm
    row = lambda b, t: (b * nt + t, 0)
    mem_spec = pl.BlockSpec((None, None) + kmem.shape[2:], lambda b, t: (layer, b, 0, 0))
    return pl.pallas_call(
        _tail_kernel,
        grid=(batch, nt),
        in_specs=[pl.BlockSpec((tm, d), row), pl.BlockSpec((tm, MIX_W), row),
                  pl.BlockSpec((tm, MEM_W), row), mem_spec, mem_spec,
                  _layer_spec(qg, layer), _const_spec(grp.shape), _layer_spec(wout, layer)]
                 + [_layer_spec(p, layer) for p in (g, wup, cw, cb, wdn)],
        out_specs=pl.BlockSpec((tm, d), row),
        out_shape=jax.ShapeDtypeStruct((n, d), F32),
        scratch_shapes=[pltpu.VMEM((SUBLANES, 2 * D_FF), F32), pltpu.VMEM((tm, D_FF), BF16)],
        compiler_params=_params(("arbitrary", "arbitrary")),
    )(x2, y, qm, kmem, vmem, qg, grp, wout, g, wup, cw, cb, wdn)


def _rope_kernel(pos_ref, inv_ref, cos_ref, sin_ref):
    ang = pos_ref[...].astype(F32) * inv_ref[...]
    cos_ref[...] = jnp.cos(ang)
    sin_ref[...] = jnp.sin(ang)


def _rope_tables(positions):
    half = MLA_ROPE // 2
    per_row = LANES // half
    n = positions.size
    rows = n // per_row
    tr = min(ROPE_ROWS, rows)
    inv = ROPE_THETA ** (-jnp.arange(half, dtype=F32) / half)
    pos = jnp.broadcast_to(positions.reshape(n, 1), (n, half)).reshape(rows, LANES)
    spec = pl.BlockSpec((tr, LANES), lambda i: (i, 0))
    cos, sin = pl.pallas_call(
        _rope_kernel,
        grid=(rows // tr,),
        in_specs=[spec, _const_spec((1, LANES))],
        out_specs=[spec, spec],
        out_shape=[jax.ShapeDtypeStruct((rows, LANES), F32)] * 2,
        compiler_params=_params(("arbitrary",)),
    )(pos, jnp.tile(inv, per_row)[None, :])
    return jnp.concatenate([cos.reshape(n, half), sin.reshape(n, half)], axis=1)


def _rope_expand():
    half = MLA_ROPE // 2
    e = np.zeros((2 * half, 3 * LANES), np.float32)
    j = np.arange(half)
    for blk in range(2):
        e[j, blk * half + j] = 1.0
        e[half + j, (2 + blk) * half + j] = 1.0
    for blk in range(4):
        e[j, LANES + blk * half + j] = 1.0
        e[half + j, 2 * LANES + blk * half + j] = -1.0 if blk % 2 == 0 else 1.0
    return e


def _mla_prep_kernel(x_ref, cs_ref, g_ref, win_ref, exp_ref, qag_ref, kvag_ref, wuq_ref, wukv_ref,
                     qgn_ref, qgr_ref, kgn_ref, kgr_ref,
                     q_ref, k_ref, v_ref, qm_ref):
    subs = [slice(r * MLA_SUB, (r + 1) * MLA_SUB) for r in range(x_ref.shape[0] // MLA_SUB)]
    c0 = MLA_Q_LORA + MLA_KV_LORA
    zs = [_dot(_rms(x_ref[rows, :], g_ref[...]).astype(BF16), win_ref[...]) for rows in subs]
    qfs = [_dot(_rms(z[:, 0:MLA_Q_LORA], qag_ref[...]).astype(BF16), wuq_ref[...]) for z in zs]
    kvfs = [_dot(_rms(z[:, MLA_Q_LORA:c0], kvag_ref[...]).astype(BF16), wukv_ref[...]) for z in zs]

    for rows, z, qf, kvf in zip(subs, zs, qfs, kvfs):
        kpe2 = z[:, c0:c0 + LANES]
        qm_ref[rows, :] = z[:, c0 + LANES:]

        hi, lo = _split2(cs_ref[rows, :])
        rope = _dot(hi, exp_ref[...]) + _dot(lo, exp_ref[...])
        tq = rope[:, 0:LANES] * qgr_ref[...]
        cos = rope[:, LANES:2 * LANES]
        sin_signed = rope[:, 2 * LANES:3 * LANES]

        for hd in range(MLA_HEADS):
            qn = qf[:, hd * MLA_PAD:hd * MLA_PAD + MLA_NOPE]
            qr = qf[:, hd * MLA_PAD + MLA_NOPE:(hd + 1) * MLA_PAD]
            ss = jnp.sum(qn * qn, axis=-1, keepdims=True) + 0.5 * jnp.sum(qr * qr, axis=-1, keepdims=True)
            rs = lax.rsqrt(ss * (1.0 / MLA_QK) + EPS) * (MLA_QK ** -0.5 * LOG2E)
            q_ref[rows, hd * MLA_PAD:hd * MLA_PAD + MLA_NOPE] = (qn * rs * qgn_ref[...]).astype(BF16)
            q_ref[rows, hd * MLA_PAD + MLA_NOPE:(hd + 1) * MLA_PAD] = (qr * rs * tq).astype(BF16)

        pk = kpe2 * kgr_ref[...]
        kr = pk * cos + pltpu.roll(pk, MLA_ROPE // 2, axis=1) * sin_signed
        sk = 0.5 * jnp.sum(kpe2 * kpe2, axis=-1, keepdims=True)
        for hd in range(MLA_HEADS):
            kn = kvf[:, hd * MLA_PAD:hd * MLA_PAD + MLA_NOPE]
            ss = jnp.sum(kn * kn, axis=-1, keepdims=True) + sk
            rs = lax.rsqrt(ss * (1.0 / MLA_QK) + EPS)
            k_ref[rows, hd * MLA_PAD:hd * MLA_PAD + MLA_NOPE] = (kn * rs * kgn_ref[...]).astype(BF16)
            k_ref[rows, hd * MLA_PAD + MLA_NOPE:(hd + 1) * MLA_PAD] = (kr * rs).astype(BF16)
            v_ref[rows, hd * MLA_PAD:hd * MLA_PAD + MLA_VDIM] = kvf[:, hd * MLA_PAD + MLA_NOPE:(hd + 1) * MLA_PAD].astype(BF16)
            v_ref[rows, hd * MLA_PAD + MLA_VDIM:(hd + 1) * MLA_PAD] = jnp.ones((MLA_SUB, MLA_PAD - MLA_VDIM), BF16)


def _mla_prep(x2, cs, g, win, expand, qag, kvag, wuq, wukv, tabs, tm):
    n, d = x2.shape
    row = lambda i: (i, 0)
    consts = [g, win, expand, qag, kvag, wuq, wukv] + list(tabs)
    return pl.pallas_call(
        _mla_prep_kernel,
        grid=(n // tm,),
        in_specs=[pl.BlockSpec((tm, d), row), pl.BlockSpec((tm, MLA_ROPE), row)]
                 + [_const_spec(c.shape) for c in consts],
        out_specs=[pl.BlockSpec((tm, MLA_HEADS * MLA_PAD), row), pl.BlockSpec((tm, MLA_HEADS * MLA_PAD), row),
                   pl.BlockSpec((tm, MLA_HEADS * MLA_PAD), row), pl.BlockSpec((tm, MEM_W), row)],
        out_shape=[jax.ShapeDtypeStruct((n, MLA_HEADS * MLA_PAD), BF16),
                   jax.ShapeDtypeStruct((n, MLA_HEADS * MLA_PAD), BF16),
                   jax.ShapeDtypeStruct((n, MLA_HEADS * MLA_PAD), BF16),
                   jax.ShapeDtypeStruct((n, MEM_W), F32)],
        compiler_params=_params(("arbitrary",)),
    )(x2, cs, *consts)


def _flash_kernel(qa_ref, qb_ref, k_ref, v_ref, o_ref):
    tq = qa_ref.shape[0]
    n_tiles = k_ref.shape[0] // tq
    r = lax.broadcasted_iota(jnp.int32, (tq, tq), 0)
    c = lax.broadcasted_iota(jnp.int32, (tq, tq), 1)
    heads = range(FLASH_HEADS)
    kc = [slice(h * MLA_PAD, (h + 1) * MLA_PAD) for h in heads]
    vc = [slice(h * MLA_VDIM, (h + 1) * MLA_VDIM) for h in heads]

    def attend(tiles):
        chains = [(q_ref, qi, h) for q_ref, qi in tiles for h in heads]
        sd = [_dot_nt(q_ref[:, kc[h]], k_ref[qi * tq:(qi + 1) * tq, kc[h]]) for q_ref, qi, h in chains]
        sp = [_dot_nt(q_ref[:, kc[h]], k_ref[0:qi * tq, kc[h]]) if qi > 0 else None for q_ref, qi, h in chains]
        pd, pp = [], []
        for n, (q_ref, qi, h) in enumerate(chains):
            s = jnp.where(c <= r, sd[n], -jnp.inf)
            m = jnp.max(s, axis=-1, keepdims=True)
            if qi > 0:
                m = jnp.maximum(m, jnp.max(sp[n], axis=-1, keepdims=True))
                pp.append(jnp.exp2((sp[n] - m).astype(BF16)))
            else:
                pp.append(None)
            pd.append(jnp.exp2((s - m).astype(BF16)))
        for n, (q_ref, qi, h) in enumerate(chains):
            acc = _dot(pd[n], v_ref[qi * tq:(qi + 1) * tq, kc[h]])
            if qi > 0:
                acc = acc + _dot(pp[n], v_ref[0:qi * tq, kc[h]])
            o_ref[qi * tq:(qi + 1) * tq, vc[h]] = (acc[:, 0:MLA_VDIM] / acc[:, MLA_VDIM:]).astype(BF16)

    for step in range((n_tiles + 1) // 2):
        late = n_tiles - 1 - step
        tiles = [(qa_ref, step)] + ([(qb_ref, late)] if late != step else [])
        pl.when(pl.program_id(2) == step)(functools.partial(attend, tiles))


def _flash(q, k, v, batch, seq, tq):
    q3 = q.reshape(batch, seq, MLA_HEADS * MLA_PAD)
    k3 = k.reshape(batch, seq, MLA_HEADS * MLA_PAD)
    v3 = v.reshape(batch, seq, MLA_HEADS * MLA_PAD)
    wq = FLASH_HEADS * MLA_PAD
    wv = FLASH_HEADS * MLA_VDIM
    n_tiles = seq // tq
    out = pl.pallas_call(
        _flash_kernel,
        grid=(batch, MLA_HEADS // FLASH_HEADS, (n_tiles + 1) // 2),
        in_specs=[pl.BlockSpec((None, tq, wq), lambda b, h, i: (b, i, h)),
                  pl.BlockSpec((None, tq, wq), lambda b, h, i: (b, n_tiles - 1 - i, h)),
                  pl.BlockSpec((None, seq, wq), lambda b, h, i: (b, 0, h)),
                  pl.BlockSpec((None, seq, wq), lambda b, h, i: (b, 0, h))],
        out_specs=pl.BlockSpec((None, seq, wv), lambda b, h, i: (b, 0, h)),
        out_shape=jax.ShapeDtypeStruct((batch, seq, MLA_HEADS * MLA_VDIM), BF16),
        compiler_params=_params(("arbitrary", "arbitrary", "arbitrary")),
    )(q3, q3, k3, v3)
    return out.reshape(batch * seq, MLA_HEADS * MLA_VDIM)


def _mla_layouts(w_in, w_uq, q_norm_g, k_norm_g):
    half = MLA_ROPE // 2
    c0 = MLA_Q_LORA + MLA_KV_LORA
    kpe = w_in[:, c0:c0 + MLA_ROPE]
    win = jnp.concatenate([w_in[:, :c0], kpe, kpe, w_in[:, c0 + MLA_ROPE:]], axis=1)
    wq = w_uq.reshape(MLA_Q_LORA, MLA_HEADS, MLA_QK)
    x1 = wq[:, :, MLA_NOPE:MLA_NOPE + half]
    x2 = wq[:, :, MLA_NOPE + half:]
    wuq = jnp.concatenate([wq[:, :, :MLA_NOPE], x1, x2, x2, x1], axis=2).reshape(MLA_Q_LORA, MLA_HEADS * MLA_PAD)
    g1, g2 = q_norm_g[MLA_NOPE:MLA_NOPE + half], q_norm_g[MLA_NOPE + half:]
    qgn = q_norm_g[:MLA_NOPE][None, :]
    qgr = jnp.concatenate([g1, g2, -g2, g1])[None, :]
    kgn = k_norm_g[:MLA_NOPE][None, :]
    kgr = jnp.concatenate([k_norm_g[MLA_NOPE:], k_norm_g[MLA_NOPE:]])[None, :]
    return win, wuq, (qgn, qgr, kgn, kgr)


def kernel(x, mem, positions, mix_norm_g, ffn_norm_g, mem_norm_g, w_mem_kv, mem_q_norm_g, mem_k_norm_g, w_out, w_up, conv_w, conv_b, w_down, hg_w_in, hg_lb_logits, hg_out_norm_g, mla_w_in, mla_qa_norm_g, mla_kva_norm_g, mla_w_uq, mla_w_ukv, mla_q_norm_g, mla_k_norm_g):
    batch, seq, d = x.shape
    n = batch * seq
    tm_big = min(512, seq)
    x2 = x.reshape(n, d)

    grp = jnp.asarray(np.kron(np.eye(MEM_HEADS), np.full((MEM_HDIM, MEM_HDIM), 1.0 / MEM_HDIM)), BF16)
    sums, tri, lev = _hg_tables()
    hg_tables = (jnp.asarray(sums, BF16), jnp.asarray(tri, BF16), jnp.asarray(lev))

    kmem, vmem = _memkv(mem, mem_norm_g[:, None, :], w_mem_kv.astype(BF16),
                        jnp.tile(mem_k_norm_g, (1, MEM_HEADS))[:, None, :], grp)
    qg = jnp.tile(mem_q_norm_g, (1, MEM_HEADS))[:, None, :]
    w_out_b, w_up_b, w_down_b = w_out.astype(BF16), w_up.astype(BF16), w_down.astype(BF16)

    def finish_layer(xin, y, qm, layer):
        return _layer_tail(xin, y, qm, kmem, vmem, layer, qg, grp, w_out_b, ffn_norm_g[:, None, :],
                           w_up_b, conv_w, conv_b[:, None, :], w_down_b, batch, seq, min(1024, seq))

    qs, lf, kk, v, sg, qm = _hg_inproj(x2, mix_norm_g[0][None, :], hg_w_in[0].astype(BF16),
                                        hg_lb_logits, tm_big)
    y = _hgrn(qs, lf, kk, v, sg, hg_tables, hg_out_norm_g[0][None, :], batch, seq)
    x2 = finish_layer(x2, y, qm, 0)

    win, wuq, tabs = _mla_layouts(mla_w_in[0], mla_w_uq[0], mla_q_norm_g[0], mla_k_norm_g[0])
    q, k, v, qm = _mla_prep(x2, _rope_tables(positions), mix_norm_g[1][None, :], win.astype(BF16),
                            jnp.asarray(_rope_expand(), BF16),
                            mla_qa_norm_g[0][None, :], mla_kva_norm_g[0][None, :], wuq.astype(BF16),
                            mla_w_ukv[0].astype(BF16), tabs, min(1024, seq))
    y = _flash(q, k, v, batch, seq, min(FLASH_TQ, seq))
    x2 = finish_layer(x2, y, qm, 1)
    return x2.reshape(batch, seq, d)
```

```python
import functools

import numpy as np
import jax
import jax.numpy as jnp
from jax import lax
from jax.experimental import pallas as pl
from jax.experimental.pallas import tpu as pltpu

F32 = jnp.float32
BF16 = jnp.bfloat16
EPS = 1e-6

HG_HEADS = 6
HG_KDIM = 128
HG_VDIM = 128
HG_F = HG_HEADS * HG_KDIM
MLA_HEADS = 6
MLA_Q_LORA = 384
MLA_KV_LORA = 256
MLA_NOPE = 128
MLA_ROPE = 64
MLA_QK = MLA_NOPE + MLA_ROPE
MLA_VDIM = 128
MLA_PAD = 256
ROPE_THETA = 10000.0
MEM_HEADS = 4
MEM_HDIM = 64
MEM_W = MEM_HEADS * MEM_HDIM
MIX_W = 768
D_FF = 2816

LANES = 128
SUBLANES = 8
VMEM_LIMIT = 56 * 1024 * 1024

HG_CHUNK = 128
HG_LEVELS = 7
HG_BIG_LEVELS = 4
HG_STEP_CHUNKS = 4
FFN_CHUNK = 256
LOG2E = 1.4426950408889634
FLASH_HEADS = 2
FLASH_TQ = 512
ROPE_ROWS = 512
ROPE_BLOCK = 512
MLA_SUB = 256


def _dot(a, b):
    return jnp.dot(a, b, preferred_element_type=F32)


def _dot_nt(a, b):
    return lax.dot_general(a, b, (((1,), (1,)), ((), ())), preferred_element_type=F32)


def _dot_tn(a, b):
    return lax.dot_general(a, b, (((0,), (0,)), ((), ())), preferred_element_type=F32)


def _rms(x, g):
    ms = jnp.mean(x * x, axis=-1, keepdims=True)
    return x * lax.rsqrt(ms + EPS) * g


def _sigmoid(x):
    return 1.0 / (1.0 + jnp.exp(-x))


def _split2(x):
    hi = x.astype(BF16)
    lo = (x - hi.astype(F32)).astype(BF16)
    return hi, lo


def _split3(x):
    hi = x.astype(BF16)
    r = x - hi.astype(F32)
    mid = r.astype(BF16)
    lo = (r - mid.astype(F32)).astype(BF16)
    return hi, mid, lo


def _const_spec(shape):
    nd = len(shape)
    return pl.BlockSpec(shape, lambda *_: (0,) * nd, pipeline_mode=pl.Buffered(1))


def _layer_spec(arr, layer):
    nd = arr.ndim - 1
    return pl.BlockSpec((None,) + arr.shape[1:], lambda *_: (layer,) + (0,) * nd,
                        pipeline_mode=pl.Buffered(1))


def _params(sem):
    return pltpu.CompilerParams(dimension_semantics=sem, vmem_limit_bytes=VMEM_LIMIT)


def _hg_inproj_kernel(x_ref, g_ref, w_ref, lbl_ref, qs_ref, lf_ref, kk_ref, v_ref, sg_ref, qm_ref):
    h = _rms(x_ref[...], g_ref[...]).astype(BF16)
    l = lbl_ref[...]
    e = jnp.exp(l - jnp.max(l, axis=0, keepdims=True))
    lb = e[0:1, :] / jnp.sum(e, axis=0, keepdims=True)
    q = _dot(h, w_ref[:, 0:HG_F])
    qs_ref[...] = q * _sigmoid(q) * (HG_KDIM ** -0.5)
    f = _dot(h, w_ref[:, HG_F:2 * HG_F])
    fg = lb + (1.0 - lb) * _sigmoid(f)
    lf_ref[...] = jnp.log2(fg)
    kk_ref[...] = 1.0 - fg
    v_ref[...] = _dot(h, w_ref[:, 2 * HG_F:3 * HG_F]).astype(BF16)
    gate = _dot(h, w_ref[:, 3 * HG_F:4 * HG_F])
    sg_ref[...] = gate * _sigmoid(gate)
    qm_ref[...] = _dot(h, w_ref[:, 4 * HG_F:4 * HG_F + MEM_W])


def _hg_inproj(x2, g, w, lb_logits, tm):
    n, d = x2.shape
    row = lambda i: (i, 0)
    out768 = pl.BlockSpec((tm, HG_F), row)
    return pl.pallas_call(
        _hg_inproj_kernel,
        grid=(n // tm,),
        in_specs=[pl.BlockSpec((tm, d), row), _const_spec((1, d)), _const_spec(w.shape),
                  _const_spec(lb_logits.shape)],
        out_specs=[out768, out768, out768, out768, out768, pl.BlockSpec((tm, MEM_W), row)],
        out_shape=[jax.ShapeDtypeStruct((n, HG_F), F32), jax.ShapeDtypeStruct((n, HG_F), F32),
                   jax.ShapeDtypeStruct((n, HG_F), F32), jax.ShapeDtypeStruct((n, HG_F), BF16),
                   jax.ShapeDtypeStruct((n, HG_F), F32), jax.ShapeDtypeStruct((n, MEM_W), F32)],
        compiler_params=_params(("arbitrary",)),
    )(x2, g, w, lb_logits)


def _memkv_kernel(mem_ref, g_ref, w_ref, kg_ref, grp_ref, k_ref, v_ref):
    mn = _rms(mem_ref[...], g_ref[...]).astype(BF16)
    kv = _dot(mn, w_ref[...])
    k = kv[:, :MEM_W]
    hi, lo = _split2(k * k)
    ms = _dot(hi, grp_ref[...]) + _dot(lo, grp_ref[...])
    k_ref[...] = (k * lax.rsqrt(ms + EPS) * kg_ref[...]).astype(BF16)
    v_ref[...] = kv[:, MEM_W:].astype(BF16)


def _memkv(mem, g, w, kg, grp):
    depth = g.shape[0]
    b, m, d = mem.shape
    spec_kv = pl.BlockSpec((None, None, m, MEM_W), lambda l, i: (l, i, 0, 0))
    return pl.pallas_call(
        _memkv_kernel,
        grid=(depth, b),
        in_specs=[pl.BlockSpec((None, m, d), lambda l, i: (i, 0, 0)),
                  pl.BlockSpec((None, 1, d), lambda l, i: (l, 0, 0)),
                  pl.BlockSpec((None, d, 2 * MEM_W), lambda l, i: (l, 0, 0)),
                  pl.BlockSpec((None, 1, MEM_W), lambda l, i: (l, 0, 0)),
                  _const_spec(grp.shape)],
        out_specs=[spec_kv, spec_kv],
        out_shape=[jax.ShapeDtypeStruct((depth, b, m, MEM_W), BF16)] * 2,
        compiler_params=_params(("arbitrary", "arbitrary")),
    )(mem, g, w, kg, grp)


def _hg_tables():
    C = HG_CHUNK
    t = np.arange(C)[:, None]
    u = np.arange(C)[None, :]
    small = []
    for l in range(HG_BIG_LEVELS, HG_LEVELS):
        c = C >> (l + 1)
        mid = (t // (2 * c)) * (2 * c) + c
        small.append(np.where(t >= mid, (u >= mid) & (u <= t), (u > t) & (u < mid)))
    tri = (u <= t).astype(np.float32)
    sums = np.concatenate([tri] + small, axis=0).astype(np.float32)
    sums = np.concatenate([sums, sums], axis=1)
    p = np.floor(np.log2(np.maximum(t ^ u, 1))).astype(np.int32)
    lev = np.where(t > u, HG_LEVELS - 1 - p, np.where(t == u, HG_LEVELS, -1)).astype(np.int32)
    return sums, tri, lev


def _hgrn_kernel(qs_ref, lf_ref, kk_ref, v_ref, sg_ref, sums_ref, tri_ref, lev_ref, og_ref, y_ref, st_ref):
    C = HG_CHUNK
    heads = range(HG_HEADS)
    cols = [slice(h * HG_KDIM, (h + 1) * HG_KDIM) for h in heads]
    chunks = [slice(s * C, (s + 1) * C) for s in range(qs_ref.shape[0] // C)]

    @pl.when(pl.program_id(1) == 0)
    def _():
        st_ref[...] = jnp.zeros_like(st_ref)

    lev = lev_ref[...]
    masks = [lev == l for l in range(HG_LEVELS + 1)]

    pairs, inter, diag = {}, {}, {}
    for s, rows in enumerate(chunks):
        hi, mid, lo = _split3(lf_ref[rows, :])
        d_all = _dot(sums_ref[...], jnp.concatenate([hi, mid], axis=0))
        b = d_all[0:C, :] + _dot(tri_ref[...], lo)
        d_small = d_all[C:, :]
        for h in heads:
            q = qs_ref[rows, cols[h]]
            k = kk_ref[rows, cols[h]]
            diag[s, h] = jnp.sum(q * k, axis=-1, keepdims=True)
            qb = q.astype(BF16)
            kb = k.astype(BF16)
            bh = b[:, cols[h]]
            ops = []
            for l in range(HG_LEVELS):
                if l < HG_BIG_LEVELS:
                    c = C >> (l + 1)
                    parts = []
                    for lo_row in range(0, C, 2 * c):
                        r = jnp.broadcast_to(bh[lo_row + c - 1:lo_row + c, :], (c, HG_KDIM))
                        parts += [r - bh[lo_row:lo_row + c, :], bh[lo_row + c:lo_row + 2 * c, :] - r]
                    e = jnp.exp2(jnp.concatenate(parts, axis=0))
                else:
                    e = jnp.exp2(d_small[(l - HG_BIG_LEVELS) * C:(l - HG_BIG_LEVELS + 1) * C, cols[h]])
                eb = e.astype(BF16)
                ops.append((qb * eb, kb * eb))
            pairs[s, h] = ops
            e_pre = jnp.exp2(bh)
            e_suf = jnp.exp2(bh[C - 1:C, :] - bh)
            inter[s, h] = (qb * e_pre.astype(BF16), kb * e_suf.astype(BF16), e_pre[C - 1:C, :])

    scores = {}
    for s in range(len(chunks)):
        for h in heads:
            a = jnp.where(masks[HG_LEVELS], diag[s, h], 0.0)
            for l in range(HG_LEVELS):
                a = jnp.where(masks[l], _dot_nt(*pairs[s, h][l]), a)
            scores[s, h] = a.astype(BF16)

    for s, rows in enumerate(chunks):
        for h in heads:
            qe, ke, dec = inter[s, h]
            v = v_ref[rows, cols[h]]
            st = st_ref[h]
            o = _dot_nt(qe, st.astype(BF16)) + _dot(scores[s, h], v)
            st_ref[h] = st * dec + _dot_tn(v, ke)
            y_ref[rows, cols[h]] = (_rms(o, og_ref[...]) * sg_ref[rows, cols[h]]).astype(BF16)


def _hgrn(qs, lf, kk, v, sg, tables, og, batch, seq):
    n = qs.shape[0]
    step = HG_CHUNK * HG_STEP_CHUNKS
    ns = seq // step
    row = lambda b, c: (b * ns + c, 0)
    spec = pl.BlockSpec((step, HG_F), row)
    return pl.pallas_call(
        _hgrn_kernel,
        grid=(batch, ns),
        in_specs=[spec, spec, spec, spec, spec] + [_const_spec(t.shape) for t in tables]
                 + [_const_spec(og.shape)],
        out_specs=spec,
        out_shape=jax.ShapeDtypeStruct((n, HG_F), BF16),
        scratch_shapes=[pltpu.VMEM((HG_HEADS, HG_VDIM, HG_KDIM), F32)],
        compiler_params=_params(("arbitrary", "arbitrary")),
    )(qs, lf, kk, v, sg, *tables, og)


def _mix_out(x, y_ref, qm_ref, km_ref, vm_ref, qg_ref, grp_ref, w_ref):
    q = qm_ref[...]
    hi, lo = _split2(q * q)
    ms = _dot(hi, grp_ref[...]) + _dot(lo, grp_ref[...])
    qn = (q * lax.rsqrt(ms + EPS) * (qg_ref[...] * (MEM_HDIM ** -0.5 * LOG2E))).astype(BF16)
    tm = qn.shape[0]
    head = lax.broadcasted_iota(jnp.int32, qn.shape, 1) // MEM_HDIM
    qs = jnp.concatenate([jnp.where(head == h, qn, jnp.zeros_like(qn)) for h in range(MEM_HEADS)], axis=0)
    s = _dot_nt(qs, km_ref[...])
    p = jnp.exp2(s - jnp.max(s, axis=-1, keepdims=True))
    o = _dot(p.astype(BF16), vm_ref[...]) / jnp.sum(p, axis=-1, keepdims=True)
    m = o[0:tm, :]
    for h in range(1, MEM_HEADS):
        m = jnp.where(head == h, o[h * tm:(h + 1) * tm, :], m)
    return (x + _dot(y_ref[...], w_ref[0:MIX_W, :])
            + _dot(m.astype(BF16), w_ref[MIX_W:MIX_W + MEM_W, :]))


def _shift_row(t, prev):
    r = pltpu.roll(t, 1, axis=0)
    pr = pltpu.roll(prev, 1, axis=0)
    sub = lax.broadcasted_iota(jnp.int32, pr.shape, 0)
    head = jnp.where(sub < 1, pr, r[0:SUBLANES, :])
    return jnp.concatenate([head, r[SUBLANES:, :]], axis=0)


def _conv_ffn(x, g_ref, wup_ref, cw_ref, cb_ref, wdn_ref, carry_ref, act_ref):
    tm = x.shape[0]
    h = _rms(x, g_ref[...]).astype(BF16)

    def conv_slice(c0):
        cols = slice(c0, c0 + FFN_CHUNK)
        u = _dot(h, wup_ref[:, cols])
        prev = carry_ref[:, cols]
        carry_ref[:, cols] = u[tm - SUBLANES:, :]
        w0, w1, w2 = cw_ref[0:1, cols], cw_ref[1:2, cols], cw_ref[2:3, cols]
        t0, p0 = w0 * u, w0 * prev
        t1 = _shift_row(t0, p0) + w1 * u
        p1 = pltpu.roll(p0, 1, axis=0) + w1 * prev
        return _shift_row(t1, p1) + (w2 * u + cb_ref[:, cols])

    for c in range(D_FF // FFN_CHUNK):
        a = conv_slice(c * FFN_CHUNK).astype(BF16)
        b = conv_slice(D_FF + c * FFN_CHUNK).astype(BF16)
        sig = 1.0 / (1.0 + jnp.exp2(a * (-LOG2E)))
        act_ref[:, c * FFN_CHUNK:(c + 1) * FFN_CHUNK] = a * b * sig
    return x + _dot(act_ref[...], wdn_ref[...])


def _tail_kernel(x_ref, y_ref, qm_ref, km_ref, vm_ref, qg_ref, grp_ref, wout_ref,
                 g_ref, wup_ref, cw_ref, cb_ref, wdn_ref, o_ref, carry_ref, act_ref):
    @pl.when(pl.program_id(1) == 0)
    def _():
        carry_ref[...] = jnp.zeros_like(carry_ref)

    x1 = _mix_out(x_ref[...], y_ref, qm_ref, km_ref, vm_ref, qg_ref, grp_ref, wout_ref)
    o_ref[...] = _conv_ffn(x1, g_ref, wup_ref, cw_ref, cb_ref, wdn_ref, carry_ref, act_ref)


def _layer_tail(x2, y, qm, kmem, vmem, layer, qg, grp, wout, g, wup, cw, cb, wdn, batch, seq, tm):
    n, d = x2.shape
    nt = seq // tm
    row = lambda b, t: (b * nt + t, 0)
    mem_spec = pl.BlockSpec((None, None) + kmem.shape[2:], lambda b, t: (layer, b, 0, 0))
    return pl.pallas_call(
        _tail_kernel,
        grid=(batch, nt),
        in_specs=[pl.BlockSpec((tm, d), row), pl.BlockSpec((tm, MIX_W), row),
                  pl.BlockSpec((tm, MEM_W), row), mem_spec, mem_spec,
                  _layer_spec(qg, layer), _const_spec(grp.shape), _layer_spec(wout, layer)]
                 + [_layer_spec(p, layer) for p in (g, wup, cw, cb, wdn)],
        out_specs=pl.BlockSpec((tm, d), row),
        out_shape=jax.ShapeDtypeStruct((n, d), F32),
        scratch_shapes=[pltpu.VMEM((SUBLANES, 2 * D_FF), F32), pltpu.VMEM((tm, D_FF), BF16)],
        compiler_params=_params(("arbitrary", "arbitrary")),
    )(x2, y, qm, kmem, vmem, qg, grp, wout, g, wup, cw, cb, wdn)


def _rope_kernel(pos_ref, inv_ref, cs_ref):
    pf = pos_ref[...].astype(F32)
    grp = lax.broadcasted_iota(jnp.int32, (pf.shape[0], LANES), 1) // (MLA_ROPE // 2)
    p = pf[:, 3:4]
    for a in range(2, -1, -1):
        p = jnp.where(grp == a, pf[:, a:a + 1], p)
    ang = p * inv_ref[...]
    cs_ref[:, 0:LANES] = jnp.cos(ang)
    cs_ref[:, LANES:2 * LANES] = jnp.sin(ang)


def _rope_tables(positions):
    half = MLA_ROPE // 2
    per_row = LANES // half
    n = positions.size
    rows = n // per_row
    tr = min(ROPE_ROWS, rows)
    inv = ROPE_THETA ** (-jnp.arange(half, dtype=F32) / half)
    pos = positions.reshape(n // ROPE_BLOCK, per_row, ROPE_BLOCK // per_row)
    pos = pos.transpose(0, 2, 1).reshape(rows, per_row)
    return pl.pallas_call(
        _rope_kernel,
        grid=(rows // tr,),
        in_specs=[pl.BlockSpec((tr, per_row), lambda i: (i, 0)), _const_spec((1, LANES))],
        out_specs=pl.BlockSpec((tr, 2 * LANES), lambda i: (i, 0)),
        out_shape=jax.ShapeDtypeStruct((rows, 2 * LANES), F32),
        compiler_params=_params(("arbitrary",)),
    )(pos, jnp.tile(inv, per_row)[None, :])


def _rope_expand():
    half = MLA_ROPE // 2
    per_row = LANES // half
    e = np.zeros((per_row, 2 * LANES, 3 * LANES), np.float32)
    j = np.arange(half)
    for a in range(per_row):
        cos_rows = a * half + j
        sin_rows = LANES + a * half + j
        for blk in range(2):
            e[a, cos_rows, blk * half + j] = 1.0
            e[a, sin_rows, (2 + blk) * half + j] = 1.0
        for blk in range(4):
            e[a, cos_rows, LANES + blk * half + j] = 1.0
            e[a, sin_rows, 2 * LANES + blk * half + j] = -1.0 if blk % 2 == 0 else 1.0
    return e


def _mla_prep_kernel(x_ref, cs_ref, g_ref, win_ref, exp_ref, qag_ref, kvag_ref, wuq_ref, wukv_ref,
                     qgn_ref, qgr_ref, kgn_ref, kgr_ref,
                     q_ref, k_ref, v_ref, qm_ref):
    subs = [slice(r * MLA_SUB, (r + 1) * MLA_SUB) for r in range(x_ref.shape[0] // MLA_SUB)]
    c0 = MLA_Q_LORA + MLA_KV_LORA
    zs = [_dot(_rms(x_ref[rows, :], g_ref[...]).astype(BF16), win_ref[...]) for rows in subs]
    qfs = [_dot(_rms(z[:, 0:MLA_Q_LORA], qag_ref[...]).astype(BF16), wuq_ref[...]) for z in zs]
    kvfs = [_dot(_rms(z[:, MLA_Q_LORA:c0], kvag_ref[...]).astype(BF16), wukv_ref[...]) for z in zs]

    for rows, z, qf, kvf in zip(subs, zs, qfs, kvfs):
        kpe2 = z[:, c0:c0 + LANES]
        qm_ref[rows, :] = z[:, c0 + LANES:]

        blk, part = divmod(rows.start, ROPE_BLOCK)
        drows = slice(blk * (ROPE_BLOCK // 4), (blk + 1) * (ROPE_BLOCK // 4))
        hi, lo = _split2(cs_ref[drows, :])
        rope = jnp.concatenate(
            [_dot(hi, exp_ref[a]) + _dot(lo, exp_ref[a])
             for a in range(part // 128, (part + MLA_SUB) // 128)], axis=0)
        tq = rope[:, 0:LANES] * qgr_ref[...]
        cos = rope[:, LANES:2 * LANES]
        sin_signed = rope[:, 2 * LANES:3 * LANES]

        for hd in range(MLA_HEADS):
            qn = qf[:, hd * MLA_PAD:hd * MLA_PAD + MLA_NOPE]
            qr = qf[:, hd * MLA_PAD + MLA_NOPE:(hd + 1) * MLA_PAD]
            ss = jnp.sum(qn * qn, axis=-1, keepdims=True) + 0.5 * jnp.sum(qr * qr, axis=-1, keepdims=True)
            rs = lax.rsqrt(ss * (1.0 / MLA_QK) + EPS) * (MLA_QK ** -0.5 * LOG2E)
            q_ref[rows, hd * MLA_PAD:hd * MLA_PAD + MLA_NOPE] = (qn * rs * qgn_ref[...]).astype(BF16)
            q_ref[rows, hd * MLA_PAD + MLA_NOPE:(hd + 1) * MLA_PAD] = (qr * rs * tq).astype(BF16)

        pk = kpe2 * kgr_ref[...]
        kr = pk * cos + pltpu.roll(pk, MLA_ROPE // 2, axis=1) * sin_signed
        sk = 0.5 * jnp.sum(kpe2 * kpe2, axis=-1, keepdims=True)
        for hd in range(MLA_HEADS):
            kn = kvf[:, hd * MLA_PAD:hd * MLA_PAD + MLA_NOPE]
            ss = jnp.sum(kn * kn, axis=-1, keepdims=True) + sk
            rs = lax.rsqrt(ss * (1.0 / MLA_QK) + EPS)
            k_ref[rows, hd * MLA_PAD:hd * MLA_PAD + MLA_NOPE] = (kn * rs * kgn_ref[...]).astype(BF16)
            k_ref[rows, hd * MLA_PAD + MLA_NOPE:(hd + 1) * MLA_PAD] = (kr * rs).astype(BF16)
            v_ref[rows, hd * MLA_PAD:hd * MLA_PAD + MLA_VDIM] = kvf[:, hd * MLA_PAD + MLA_NOPE:(hd + 1) * MLA_PAD].astype(BF16)
            v_ref[rows, hd * MLA_PAD + MLA_VDIM:(hd + 1) * MLA_PAD] = jnp.ones((MLA_SUB, MLA_PAD - MLA_VDIM), BF16)


def _mla_prep(x2, cs, g, win, expand, qag, kvag, wuq, wukv, tabs, tm):
    n, d = x2.shape
    row = lambda i: (i, 0)
    consts = [g, win, expand, qag, kvag, wuq, wukv] + list(tabs)
    return pl.pallas_call(
        _mla_prep_kernel,
        grid=(n // tm,),
        in_specs=[pl.BlockSpec((tm, d), row), pl.BlockSpec((tm // 4, 2 * LANES), row)]
                 + [_const_spec(c.shape) for c in consts],
        out_specs=[pl.BlockSpec((tm, MLA_HEADS * MLA_PAD), row), pl.BlockSpec((tm, MLA_HEADS * MLA_PAD), row),
                   pl.BlockSpec((tm, MLA_HEADS * MLA_PAD), row), pl.BlockSpec((tm, MEM_W), row)],
        out_shape=[jax.ShapeDtypeStruct((n, MLA_HEADS * MLA_PAD), BF16),
                   jax.ShapeDtypeStruct((n, MLA_HEADS * MLA_PAD), BF16),
                   jax.ShapeDtypeStruct((n, MLA_HEADS * MLA_PAD), BF16),
                   jax.ShapeDtypeStruct((n, MEM_W), F32)],
        compiler_params=_params(("arbitrary",)),
    )(x2, cs, *consts)


def _flash_kernel(qa_ref, qb_ref, k_ref, v_ref, o_ref):
    tq = qa_ref.shape[0]
    n_tiles = k_ref.shape[0] // tq
    r = lax.broadcasted_iota(jnp.int32, (tq, tq), 0)
    c = lax.broadcasted_iota(jnp.int32, (tq, tq), 1)
    heads = range(FLASH_HEADS)
    kc = [slice(h * MLA_PAD, (h + 1) * MLA_PAD) for h in heads]
    vc = [slice(h * MLA_VDIM, (h + 1) * MLA_VDIM) for h in heads]

    def attend(tiles):
        chains = [(q_ref, qi, h) for q_ref, qi in tiles for h in heads]
        sd = [_dot_nt(q_ref[:, kc[h]], k_ref[qi * tq:(qi + 1) * tq, kc[h]]) for q_ref, qi, h in chains]
        sp = [_dot_nt(q_ref[:, kc[h]], k_ref[0:qi * tq, kc[h]]) if qi > 0 else None for q_ref, qi, h in chains]
        pd, pp = [], []
        for n, (q_ref, qi, h) in enumerate(chains):
            s = jnp.where(c <= r, sd[n], -jnp.inf)
            m = jnp.max(s, axis=-1, keepdims=True)
            if qi > 0:
                m = jnp.maximum(m, jnp.max(sp[n], axis=-1, keepdims=True))
                pp.append(jnp.exp2((sp[n] - m).astype(BF16)))
            else:
                pp.append(None)
            pd.append(jnp.exp2((s - m).astype(BF16)))
        for n, (q_ref, qi, h) in enumerate(chains):
            acc = _dot(pd[n], v_ref[qi * tq:(qi + 1) * tq, kc[h]])
            if qi > 0:
                acc = acc + _dot(pp[n], v_ref[0:qi * tq, kc[h]])
            o_ref[qi * tq:(qi + 1) * tq, vc[h]] = (acc[:, 0:MLA_VDIM] / acc[:, MLA_VDIM:]).astype(BF16)

    for step in range((n_tiles + 1) // 2):
        late = n_tiles - 1 - step
        tiles = [(qa_ref, step)] + ([(qb_ref, late)] if late != step else [])
        pl.when(pl.program_id(2) == step)(functools.partial(attend, tiles))


def _flash(q, k, v, batch, seq, tq):
    q3 = q.reshape(batch, seq, MLA_HEADS * MLA_PAD)
    k3 = k.reshape(batch, seq, MLA_HEADS * MLA_PAD)
    v3 = v.reshape(batch, seq, MLA_HEADS * MLA_PAD)
    wq = FLASH_HEADS * MLA_PAD
    wv = FLASH_HEADS * MLA_VDIM
    n_tiles = seq // tq
    out = pl.pallas_call(
        _flash_kernel,
        grid=(batch, MLA_HEADS // FLASH_HEADS, (n_tiles + 1) // 2),
        in_specs=[pl.BlockSpec((None, tq, wq), lambda b, h, i: (b, i, h)),
                  pl.BlockSpec((None, tq, wq), lambda b, h, i: (b, n_tiles - 1 - i, h)),
                  pl.BlockSpec((None, seq, wq), lambda b, h, i: (b, 0, h)),
                  pl.BlockSpec((None, seq, wq), lambda b, h, i: (b, 0, h))],
        out_specs=pl.BlockSpec((None, seq, wv), lambda b, h, i: (b, 0, h)),
        out_shape=jax.ShapeDtypeStruct((batch, seq, MLA_HEADS * MLA_VDIM), BF16),
        compiler_params=_params(("arbitrary", "arbitrary", "arbitrary")),
    )(q3, q3, k3, v3)
    return out.reshape(batch * seq, MLA_HEADS * MLA_VDIM)


def _mla_layouts(w_in, w_uq, q_norm_g, k_norm_g):
    half = MLA_ROPE // 2
    c0 = MLA_Q_LORA + MLA_KV_LORA
    kpe = w_in[:, c0:c0 + MLA_ROPE]
    win = jnp.concatenate([w_in[:, :c0], kpe, kpe, w_in[:, c0 + MLA_ROPE:]], axis=1)
    wq = w_uq.reshape(MLA_Q_LORA, MLA_HEADS, MLA_QK)
    x1 = wq[:, :, MLA_NOPE:MLA_NOPE + half]
    x2 = wq[:, :, MLA_NOPE + half:]
    wuq = jnp.concatenate([wq[:, :, :MLA_NOPE], x1, x2, x2, x1], axis=2).reshape(MLA_Q_LORA, MLA_HEADS * MLA_PAD)
    g1, g2 = q_norm_g[MLA_NOPE:MLA_NOPE + half], q_norm_g[MLA_NOPE + half:]
    qgn = q_norm_g[:MLA_NOPE][None, :]
    qgr = jnp.concatenate([g1, g2, -g2, g1])[None, :]
    kgn = k_norm_g[:MLA_NOPE][None, :]
    kgr = jnp.concatenate([k_norm_g[MLA_NOPE:], k_norm_g[MLA_NOPE:]])[None, :]
    return win, wuq, (qgn, qgr, kgn, kgr)


def kernel(x, mem, positions, mix_norm_g, ffn_norm_g, mem_norm_g, w_mem_kv, mem_q_norm_g, mem_k_norm_g, w_out, w_up, conv_w, conv_b, w_down, hg_w_in, hg_lb_logits, hg_out_norm_g, mla_w_in, mla_qa_norm_g, mla_kva_norm_g, mla_w_uq, mla_w_ukv, mla_q_norm_g, mla_k_norm_g):
    batch, seq, d = x.shape
    n = batch * seq
    tm_big = min(512, seq)
    x2 = x.reshape(n, d)

    grp = jnp.asarray(np.kron(np.eye(MEM_HEADS), np.full((MEM_HDIM, MEM_HDIM), 1.0 / MEM_HDIM)), BF16)
    sums, tri, lev = _hg_tables()
    hg_tables = (jnp.asarray(sums, BF16), jnp.asarray(tri, BF16), jnp.asarray(lev))

    kmem, vmem = _memkv(mem, mem_norm_g[:, None, :], w_mem_kv.astype(BF16),
                        jnp.tile(mem_k_norm_g, (1, MEM_HEADS))[:, None, :], grp)
    qg = jnp.tile(mem_q_norm_g, (1, MEM_HEADS))[:, None, :]
    w_out_b, w_up_b, w_down_b = w_out.astype(BF16), w_up.astype(BF16), w_down.astype(BF16)

    def finish_layer(xin, y, qm, layer):
        return _layer_tail(xin, y, qm, kmem, vmem, layer, qg, grp, w_out_b, ffn_norm_g[:, None, :],
                           w_up_b, conv_w, conv_b[:, None, :], w_down_b, batch, seq, min(1024, seq))

    qs, lf, kk, v, sg, qm = _hg_inproj(x2, mix_norm_g[0][None, :], hg_w_in[0].astype(BF16),
                                        hg_lb_logits, tm_big)
    y = _hgrn(qs, lf, kk, v, sg, hg_tables, hg_out_norm_g[0][None, :], batch, seq)
    x2 = finish_layer(x2, y, qm, 0)

    win, wuq, tabs = _mla_layouts(mla_w_in[0], mla_w_uq[0], mla_q_norm_g[0], mla_k_norm_g[0])
    q, k, v, qm = _mla_prep(x2, _rope_tables(positions), mix_norm_g[1][None, :], win.astype(BF16),
                            jnp.asarray(_rope_expand(), BF16),
                            mla_qa_norm_g[0][None, :], mla_kva_norm_g[0][None, :], wuq.astype(BF16),
                            mla_w_ukv[0].astype(BF16), tabs, min(1024, seq))
    y = _flash(q, k, v, batch, seq, min(FLASH_TQ, seq))
    x2 = finish_layer(x2, y, qm, 1)
    return x2.reshape(batch, seq, d)
```

```python
import functools

import numpy as np
import jax
import jax.numpy as jnp
from jax import lax
from jax.experimental import pallas as pl
from jax.experimental.pallas import tpu as pltpu

F32 = jnp.float32
BF16 = jnp.bfloat16
EPS = 1e-6

HG_HEADS = 6
HG_KDIM = 128
HG_VDIM = 128
HG_F = HG_HEADS * HG_KDIM
MLA_HEADS = 6
MLA_Q_LORA = 384
MLA_KV_LORA = 256
MLA_NOPE = 128
MLA_ROPE = 64
MLA_QK = MLA_NOPE + MLA_ROPE
MLA_VDIM = 128
MLA_PAD = 256
ROPE_THETA = 10000.0
MEM_HEADS = 4
MEM_HDIM = 64
MEM_W = MEM_HEADS * MEM_HDIM
MIX_W = 768
D_FF = 2816

LANES = 128
SUBLANES = 8
VMEM_LIMIT = 56 * 1024 * 1024

HG_CHUNK = 128
HG_LEVELS = 7
HG_BIG_LEVELS = 4
HG_STEP_CHUNKS = 4
FFN_CHUNK = 256
LOG2E = 1.4426950408889634
FLASH_HEADS = 2
FLASH_TQ = 512
ROPE_ROWS = 512
ROPE_BLOCK = 512
MLA_SUB = 256
ROPE_GROUP = ROPE_BLOCK // (LANES // (MLA_ROPE // 2))
TAIL_ROWS = 1024
PREP_ROWS = 1024
INPROJ_ROWS = 1024


def _dot(a, b):
    return jnp.dot(a, b, preferred_element_type=F32)


def _dot_nt(a, b):
    return lax.dot_general(a, b, (((1,), (1,)), ((), ())), preferred_element_type=F32)


def _dot_tn(a, b):
    return lax.dot_general(a, b, (((0,), (0,)), ((), ())), preferred_element_type=F32)


def _rms(x, g):
    ms = jnp.mean(x * x, axis=-1, keepdims=True)
    return x * lax.rsqrt(ms + EPS) * g


def _sigmoid(x):
    return 1.0 / (1.0 + jnp.exp(-x))


def _split2(x):
    hi = x.astype(BF16)
    lo = (x - hi.astype(F32)).astype(BF16)
    return hi, lo


def _split3(x):
    hi = x.astype(BF16)
    r = x - hi.astype(F32)
    mid = r.astype(BF16)
    lo = (r - mid.astype(F32)).astype(BF16)
    return hi, mid, lo


def _const_spec(shape):
    nd = len(shape)
    return pl.BlockSpec(shape, lambda *_: (0,) * nd, pipeline_mode=pl.Buffered(1))


def _layer_spec(arr, layer):
    nd = arr.ndim - 1
    return pl.BlockSpec((None,) + arr.shape[1:], lambda *_: (layer,) + (0,) * nd,
                        pipeline_mode=pl.Buffered(1))


def _params(sem):
    return pltpu.CompilerParams(dimension_semantics=sem, vmem_limit_bytes=VMEM_LIMIT)


def _hg_inproj_kernel(x_ref, g_ref, w_ref, lbl_ref, qs_ref, lf_ref, kk_ref, v_ref, sg_ref, qm_ref):
    h = _rms(x_ref[...], g_ref[...]).astype(BF16)
    l = lbl_ref[...]
    e = jnp.exp(l - jnp.max(l, axis=0, keepdims=True))
    lb = e[0:1, :] / jnp.sum(e, axis=0, keepdims=True)
    q = _dot(h, w_ref[:, 0:HG_F])
    qs_ref[...] = q * _sigmoid(q) * (HG_KDIM ** -0.5)
    f = _dot(h, w_ref[:, HG_F:2 * HG_F])
    fg = lb + (1.0 - lb) * _sigmoid(f)
    lf_ref[...] = jnp.log2(fg)
    kk_ref[...] = 1.0 - fg
    v_ref[...] = _dot(h, w_ref[:, 2 * HG_F:3 * HG_F]).astype(BF16)
    gate = _dot(h, w_ref[:, 3 * HG_F:4 * HG_F])
    sg_ref[...] = gate * _sigmoid(gate)
    qm_ref[...] = _dot(h, w_ref[:, 4 * HG_F:4 * HG_F + MEM_W])


def _hg_inproj(x2, g, w, lb_logits, tm):
    n, d = x2.shape
    row = lambda i: (i, 0)
    out768 = pl.BlockSpec((tm, HG_F), row)
    return pl.pallas_call(
        _hg_inproj_kernel,
        grid=(n // tm,),
        in_specs=[pl.BlockSpec((tm, d), row), _const_spec((1, d)), _const_spec(w.shape),
                  _const_spec(lb_logits.shape)],
        out_specs=[out768, out768, out768, out768, out768, pl.BlockSpec((tm, MEM_W), row)],
        out_shape=[jax.ShapeDtypeStruct((n, HG_F), F32), jax.ShapeDtypeStruct((n, HG_F), F32),
                   jax.ShapeDtypeStruct((n, HG_F), F32), jax.ShapeDtypeStruct((n, HG_F), BF16),
                   jax.ShapeDtypeStruct((n, HG_F), F32), jax.ShapeDtypeStruct((n, MEM_W), F32)],
        compiler_params=_params(("arbitrary",)),
    )(x2, g, w, lb_logits)


def _memkv_kernel(mem_ref, g_ref, w_ref, kg_ref, grp_ref, k_ref, v_ref):
    mn = _rms(mem_ref[...], g_ref[...]).astype(BF16)
    kv = _dot(mn, w_ref[...])
    k = kv[:, :MEM_W]
    hi, lo = _split2(k * k)
    ms = _dot(hi, grp_ref[...]) + _dot(lo, grp_ref[...])
    k_ref[...] = (k * lax.rsqrt(ms + EPS) * kg_ref[...]).astype(BF16)
    v_ref[...] = kv[:, MEM_W:].astype(BF16)


def _memkv(mem, g, w, kg, grp):
    depth = g.shape[0]
    b, m, d = mem.shape
    spec_kv = pl.BlockSpec((None, None, m, MEM_W), lambda l, i: (l, i, 0, 0))
    return pl.pallas_call(
        _memkv_kernel,
        grid=(depth, b),
        in_specs=[pl.BlockSpec((None, m, d), lambda l, i: (i, 0, 0)),
                  pl.BlockSpec((None, 1, d), lambda l, i: (l, 0, 0)),
                  pl.BlockSpec((None, d, 2 * MEM_W), lambda l, i: (l, 0, 0)),
                  pl.BlockSpec((None, 1, MEM_W), lambda l, i: (l, 0, 0)),
                  _const_spec(grp.shape)],
        out_specs=[spec_kv, spec_kv],
        out_shape=[jax.ShapeDtypeStruct((depth, b, m, MEM_W), BF16)] * 2,
        compiler_params=_params(("arbitrary", "arbitrary")),
    )(mem, g, w, kg, grp)


def _hg_tables():
    C = HG_CHUNK
    t = np.arange(C)[:, None]
    u = np.arange(C)[None, :]
    small = []
    for l in range(HG_BIG_LEVELS, HG_LEVELS):
        c = C >> (l + 1)
        mid = (t // (2 * c)) * (2 * c) + c
        small.append(np.where(t >= mid, (u >= mid) & (u <= t), (u > t) & (u < mid)))
    tri = (u <= t).astype(np.float32)
    sums = np.concatenate([tri] + small, axis=0).astype(np.float32)
    sums = np.concatenate([sums, sums], axis=1)
    p = np.floor(np.log2(np.maximum(t ^ u, 1))).astype(np.int32)
    lev = np.where(t > u, HG_LEVELS - 1 - p, np.where(t == u, HG_LEVELS, -1)).astype(np.int32)
    return sums, tri, lev


def _hgrn_kernel(qs_ref, lf_ref, kk_ref, v_ref, sg_ref, sums_ref, tri_ref, lev_ref, og_ref, y_ref, st_ref):
    C = HG_CHUNK
    heads = range(HG_HEADS)
    cols = [slice(h * HG_KDIM, (h + 1) * HG_KDIM) for h in heads]
    chunks = [slice(s * C, (s + 1) * C) for s in range(qs_ref.shape[0] // C)]

    @pl.when(pl.program_id(1) == 0)
    def _():
        st_ref[...] = jnp.zeros_like(st_ref)

    lev = lev_ref[...]
    masks = [lev == l for l in range(HG_LEVELS + 1)]

    pairs, inter, diag = {}, {}, {}
    for s, rows in enumerate(chunks):
        hi, mid, lo = _split3(lf_ref[rows, :])
        d_all = _dot(sums_ref[...], jnp.concatenate([hi, mid], axis=0))
        b = d_all[0:C, :] + _dot(tri_ref[...], lo)
        d_small = d_all[C:, :]
        for h in heads:
            q = qs_ref[rows, cols[h]]
            k = kk_ref[rows, cols[h]]
            diag[s, h] = jnp.sum(q * k, axis=-1, keepdims=True)
            qb = q.astype(BF16)
            kb = k.astype(BF16)
            bh = b[:, cols[h]]
            ops = []
            for l in range(HG_LEVELS):
                if l < HG_BIG_LEVELS:
                    c = C >> (l + 1)
                    parts = []
                    for lo_row in range(0, C, 2 * c):
                        r = jnp.broadcast_to(bh[lo_row + c - 1:lo_row + c, :], (c, HG_KDIM))
                        parts += [r - bh[lo_row:lo_row + c, :], bh[lo_row + c:lo_row + 2 * c, :] - r]
                    e = jnp.exp2(jnp.concatenate(parts, axis=0))
                else:
                    e = jnp.exp2(d_small[(l - HG_BIG_LEVELS) * C:(l - HG_BIG_LEVELS + 1) * C, cols[h]])
                eb = e.astype(BF16)
                ops.append((qb * eb, kb * eb))
            pairs[s, h] = ops
            e_pre = jnp.exp2(bh)
            e_suf = jnp.exp2(bh[C - 1:C, :] - bh)
            inter[s, h] = (qb * e_pre.astype(BF16), kb * e_suf.astype(BF16), e_pre[C - 1:C, :])

    scores = {}
    for s in range(len(chunks)):
        for h in heads:
            a = jnp.where(masks[HG_LEVELS], diag[s, h], 0.0)
            for l in range(HG_LEVELS):
                a = jnp.where(masks[l], _dot_nt(*pairs[s, h][l]), a)
            scores[s, h] = a.astype(BF16)

    for s, rows in enumerate(chunks):
        for h in heads:
            qe, ke, dec = inter[s, h]
            v = v_ref[rows, cols[h]]
            st = st_ref[h]
            o = _dot_nt(qe, st.astype(BF16)) + _dot(scores[s, h], v)
            st_ref[h] = st * dec + _dot_tn(v, ke)
            y_ref[rows, cols[h]] = (_rms(o, og_ref[...]) * sg_ref[rows, cols[h]]).astype(BF16)


def _hgrn(qs, lf, kk, v, sg, tables, og, batch, seq):
    n = qs.shape[0]
    step = HG_CHUNK * HG_STEP_CHUNKS
    ns = seq // step
    row = lambda b, c: (b * ns + c, 0)
    spec = pl.BlockSpec((step, HG_F), row)
    return pl.pallas_call(
        _hgrn_kernel,
        grid=(batch, ns),
        in_specs=[spec, spec, spec, spec, spec] + [_const_spec(t.shape) for t in tables]
                 + [_const_spec(og.shape)],
        out_specs=spec,
        out_shape=jax.ShapeDtypeStruct((n, HG_F), BF16),
        scratch_shapes=[pltpu.VMEM((HG_HEADS, HG_VDIM, HG_KDIM), F32)],
        compiler_params=_params(("arbitrary", "arbitrary")),
    )(qs, lf, kk, v, sg, *tables, og)


def _mix_out(x, y_ref, qm_ref, km_ref, vm_ref, qg_ref, grp_ref, w_ref):
    q = qm_ref[...]
    hi, lo = _split2(q * q)
    ms = _dot(hi, grp_ref[...]) + _dot(lo, grp_ref[...])
    qn = (q * lax.rsqrt(ms + EPS) * (qg_ref[...] * (MEM_HDIM ** -0.5 * LOG2E))).astype(BF16)
    tm = qn.shape[0]
    head = lax.broadcasted_iota(jnp.int32, qn.shape, 1) // MEM_HDIM
    qs = jnp.concatenate([jnp.where(head == h, qn, jnp.zeros_like(qn)) for h in range(MEM_HEADS)], axis=0)
    s = _dot_nt(qs, km_ref[...])
    p = jnp.exp2(s - jnp.max(s, axis=-1, keepdims=True))
    o = _dot(p.astype(BF16), vm_ref[...]) / jnp.sum(p, axis=-1, keepdims=True)
    m = o[0:tm, :]
    for h in range(1, MEM_HEADS):
        m = jnp.where(head == h, o[h * tm:(h + 1) * tm, :], m)
    return (x + _dot(y_ref[...], w_ref[0:MIX_W, :])
            + _dot(m.astype(BF16), w_ref[MIX_W:MIX_W + MEM_W, :]))


def _shift_row(t, prev):
    r = pltpu.roll(t, 1, axis=0)
    pr = pltpu.roll(prev, 1, axis=0)
    sub = lax.broadcasted_iota(jnp.int32, pr.shape, 0)
    head = jnp.where(sub < 1, pr, r[0:SUBLANES, :])
    return jnp.concatenate([head, r[SUBLANES:, :]], axis=0)


def _conv_ffn(x, g_ref, wup_ref, cw_ref, cb_ref, wdn_ref, carry_ref, act_ref):
    tm = x.shape[0]
    h = _rms(x, g_ref[...]).astype(BF16)

    def conv_slice(c0):
        cols = slice(c0, c0 + FFN_CHUNK)
        u = _dot(h, wup_ref[:, cols])
        prev = carry_ref[:, cols]
        carry_ref[:, cols] = u[tm - SUBLANES:, :]
        w0, w1, w2 = cw_ref[0:1, cols], cw_ref[1:2, cols], cw_ref[2:3, cols]
        t0, p0 = w0 * u, w0 * prev
        t1 = _shift_row(t0, p0) + w1 * u
        p1 = pltpu.roll(p0, 1, axis=0) + w1 * prev
        return _shift_row(t1, p1) + (w2 * u + cb_ref[:, cols])

    for c in range(D_FF // FFN_CHUNK):
        a = conv_slice(c * FFN_CHUNK).astype(BF16)
        b = conv_slice(D_FF + c * FFN_CHUNK).astype(BF16)
        sig = 1.0 / (1.0 + jnp.exp2(a * (-LOG2E)))
        act_ref[:, c * FFN_CHUNK:(c + 1) * FFN_CHUNK] = a * b * sig
    return x + _dot(act_ref[...], wdn_ref[...])


def _tail_kernel(x_ref, y_ref, qm_ref, km_ref, vm_ref, qg_ref, grp_ref, wout_ref,
                 g_ref, wup_ref, cw_ref, cb_ref, wdn_ref, o_ref, carry_ref, act_ref):
    @pl.when(pl.program_id(1) == 0)
    def _():
        carry_ref[...] = jnp.zeros_like(carry_ref)

    x1 = _mix_out(x_ref[...], y_ref, qm_ref, km_ref, vm_ref, qg_ref, grp_ref, wout_ref)
    o_ref[...] = _conv_ffn(x1, g_ref, wup_ref, cw_ref, cb_ref, wdn_ref, carry_ref, act_ref)


def _layer_tail(x2, y, qm, kmem, vmem, layer, qg, grp, wout, g, wup, cw, cb, wdn, batch, seq, tm):
    n, d = x2.shape
    nt = seq // tm
    row = lambda b, t: (b * nt + t, 0)
    mem_spec = pl.BlockSpec((None, None) + kmem.shape[2:], lambda b, t: (layer, b, 0, 0))
    return pl.pallas_call(
        _tail_kernel,
        grid=(batch, nt),
        in_specs=[pl.BlockSpec((tm, d), row), pl.BlockSpec((tm, MIX_W), row),
                  pl.BlockSpec((tm, MEM_W), row), mem_spec, mem_spec,
                  _layer_spec(qg, layer), _const_spec(grp.shape), _layer_spec(wout, layer)]
                 + [_layer_spec(p, layer) for p in (g, wup, cw, cb, wdn)],
        out_specs=pl.BlockSpec((tm, d), row),
        out_shape=jax.ShapeDtypeStruct((n, d), F32),
        scratch_shapes=[pltpu.VMEM((SUBLANES, 2 * D_FF), F32), pltpu.VMEM((tm, D_FF), BF16)],
        compiler_params=_params(("arbitrary", "arbitrary")),
    )(x2, y, qm, kmem, vmem, qg, grp, wout, g, wup, cw, cb, wdn)


def _rope_kernel(pos_ref, inv_ref, cs_ref):
    pf = pos_ref[...].astype(F32)
    grp = lax.broadcasted_iota(jnp.int32, (pf.shape[0], LANES), 1) // (MLA_ROPE // 2)
    p = pf[:, 3:4]
    for a in range(2, -1, -1):
        p = jnp.where(grp == a, pf[:, a:a + 1], p)
    ang = p * inv_ref[...]
    cs_ref[:, 0:LANES] = jnp.cos(ang)
    cs_ref[:, LANES:2 * LANES] = jnp.sin(ang)


def _rope_tables(positions):
    half = MLA_ROPE // 2
    per_row = LANES // half
    n = positions.size
    rows = n // per_row
    tr = min(ROPE_ROWS, rows)
    inv = ROPE_THETA ** (-jnp.arange(half, dtype=F32) / half)
    pos = positions.reshape(n // ROPE_BLOCK, per_row, ROPE_BLOCK // per_row)
    pos = pos.transpose(0, 2, 1).reshape(rows, per_row)
    return pl.pallas_call(
        _rope_kernel,
        grid=(rows // tr,),
        in_specs=[pl.BlockSpec((tr, per_row), lambda i: (i, 0)), _const_spec((1, LANES))],
        out_specs=pl.BlockSpec((tr, 2 * LANES), lambda i: (i, 0)),
        out_shape=jax.ShapeDtypeStruct((rows, 2 * LANES), F32),
        compiler_params=_params(("arbitrary",)),
    )(pos, jnp.tile(inv, per_row)[None, :])


def _rope_expand():
    half = MLA_ROPE // 2
    per_row = LANES // half
    e = np.zeros((per_row, 2 * LANES, 3 * LANES), np.float32)
    j = np.arange(half)
    for a in range(per_row):
        cos_rows = a * half + j
        sin_rows = LANES + a * half + j
        for blk in range(2):
            e[a, cos_rows, blk * half + j] = 1.0
            e[a, sin_rows, (2 + blk) * half + j] = 1.0
        for blk in range(4):
            e[a, cos_rows, LANES + blk * half + j] = 1.0
            e[a, sin_rows, 2 * LANES + blk * half + j] = -1.0 if blk % 2 == 0 else 1.0
    return e


def _mla_prep_kernel(x_ref, cs_ref, g_ref, win_ref, exp_ref, qag_ref, kvag_ref, wuq_ref, wukv_ref,
                     qgn_ref, qgr_ref, kgn_ref, kgr_ref,
                     q_ref, k_ref, v_ref, qm_ref):
    subs = [slice(r * MLA_SUB, (r + 1) * MLA_SUB) for r in range(x_ref.shape[0] // MLA_SUB)]
    c0 = MLA_Q_LORA + MLA_KV_LORA
    zs = [_dot(_rms(x_ref[rows, :], g_ref[...]).astype(BF16), win_ref[...]) for rows in subs]
    qfs = [_dot(_rms(z[:, 0:MLA_Q_LORA], qag_ref[...]).astype(BF16), wuq_ref[...]) for z in zs]
    kvfs = [_dot(_rms(z[:, MLA_Q_LORA:c0], kvag_ref[...]).astype(BF16), wukv_ref[...]) for z in zs]

    for rows, z, qf, kvf in zip(subs, zs, qfs, kvfs):
        kpe2 = z[:, c0:c0 + LANES]
        qm_ref[rows, :] = z[:, c0 + LANES:]

        blk, part = divmod(rows.start, ROPE_BLOCK)
        drows = slice(blk * ROPE_GROUP, (blk + 1) * ROPE_GROUP)
        hi, lo = _split2(cs_ref[drows, :])
        rope = jnp.concatenate(
            [_dot(hi, exp_ref[a]) + _dot(lo, exp_ref[a])
             for a in range(part // ROPE_GROUP, (part + MLA_SUB) // ROPE_GROUP)], axis=0)
        tq = rope[:, 0:LANES] * qgr_ref[...]
        cos = rope[:, LANES:2 * LANES]
        sin_signed = rope[:, 2 * LANES:3 * LANES]

        for hd in range(MLA_HEADS):
            qn = qf[:, hd * MLA_PAD:hd * MLA_PAD + MLA_NOPE]
            qr = qf[:, hd * MLA_PAD + MLA_NOPE:(hd + 1) * MLA_PAD]
            ss = jnp.sum(qn * qn, axis=-1, keepdims=True) + 0.5 * jnp.sum(qr * qr, axis=-1, keepdims=True)
            rs = lax.rsqrt(ss * (1.0 / MLA_QK) + EPS) * (MLA_QK ** -0.5 * LOG2E)
            q_ref[rows, hd * MLA_PAD:hd * MLA_PAD + MLA_NOPE] = (qn * rs * qgn_ref[...]).astype(BF16)
            q_ref[rows, hd * MLA_PAD + MLA_NOPE:(hd + 1) * MLA_PAD] = (qr * rs * tq).astype(BF16)

        pk = kpe2 * kgr_ref[...]
        kr = pk * cos + pltpu.roll(pk, MLA_ROPE // 2, axis=1) * sin_signed
        sk = 0.5 * jnp.sum(kpe2 * kpe2, axis=-1, keepdims=True)
        for hd in range(MLA_HEADS):
            kn = kvf[:, hd * MLA_PAD:hd * MLA_PAD + MLA_NOPE]
            ss = jnp.sum(kn * kn, axis=-1, keepdims=True) + sk
            rs = lax.rsqrt(ss * (1.0 / MLA_QK) + EPS)
            k_ref[rows, hd * MLA_PAD:hd * MLA_PAD + MLA_NOPE] = (kn * rs * kgn_ref[...]).astype(BF16)
            k_ref[rows, hd * MLA_PAD + MLA_NOPE:(hd + 1) * MLA_PAD] = (kr * rs).astype(BF16)
            v_ref[rows, hd * MLA_PAD:hd * MLA_PAD + MLA_VDIM] = kvf[:, hd * MLA_PAD + MLA_NOPE:(hd + 1) * MLA_PAD].astype(BF16)
            v_ref[rows, hd * MLA_PAD + MLA_VDIM:(hd + 1) * MLA_PAD] = jnp.ones((MLA_SUB, MLA_PAD - MLA_VDIM), BF16)


def _mla_prep(x2, cs, g, win, expand, qag, kvag, wuq, wukv, tabs, tm):
    n, d = x2.shape
    row = lambda i: (i, 0)
    consts = [g, win, expand, qag, kvag, wuq, wukv] + list(tabs)
    return pl.pallas_call(
        _mla_prep_kernel,
        grid=(n // tm,),
        in_specs=[pl.BlockSpec((tm, d), row), pl.BlockSpec((tm // ROPE_BLOCK * ROPE_GROUP, 2 * LANES), row)]
                 + [_const_spec(c.shape) for c in consts],
        out_specs=[pl.BlockSpec((tm, MLA_HEADS * MLA_PAD), row), pl.BlockSpec((tm, MLA_HEADS * MLA_PAD), row),
                   pl.BlockSpec((tm, MLA_HEADS * MLA_PAD), row), pl.BlockSpec((tm, MEM_W), row)],
        out_shape=[jax.ShapeDtypeStruct((n, MLA_HEADS * MLA_PAD), BF16),
                   jax.ShapeDtypeStruct((n, MLA_HEADS * MLA_PAD), BF16),
                   jax.ShapeDtypeStruct((n, MLA_HEADS * MLA_PAD), BF16),
                   jax.ShapeDtypeStruct((n, MEM_W), F32)],
        compiler_params=_params(("arbitrary",)),
    )(x2, cs, *consts)


def _flash_kernel(qa_ref, qb_ref, k_ref, v_ref, o_ref):
    tq = qa_ref.shape[0]
    n_tiles = k_ref.shape[0] // tq
    r = lax.broadcasted_iota(jnp.int32, (tq, tq), 0)
    c = lax.broadcasted_iota(jnp.int32, (tq, tq), 1)
    heads = range(FLASH_HEADS)
    kc = [slice(h * MLA_PAD, (h + 1) * MLA_PAD) for h in heads]
    vc = [slice(h * MLA_VDIM, (h + 1) * MLA_VDIM) for h in heads]

    def attend(tiles):
        chains = [(q_ref, qi, h) for q_ref, qi in tiles for h in heads]
        sd = [_dot_nt(q_ref[:, kc[h]], k_ref[qi * tq:(qi + 1) * tq, kc[h]]) for q_ref, qi, h in chains]
        sp = [_dot_nt(q_ref[:, kc[h]], k_ref[0:qi * tq, kc[h]]) if qi > 0 else None for q_ref, qi, h in chains]
        pd, pp = [], []
        for n, (q_ref, qi, h) in enumerate(chains):
            s = jnp.where(c <= r, sd[n], -jnp.inf)
            m = jnp.max(s, axis=-1, keepdims=True)
            if qi > 0:
                m = jnp.maximum(m, jnp.max(sp[n], axis=-1, keepdims=True))
                pp.append(jnp.exp2((sp[n] - m).astype(BF16)))
            else:
                pp.append(None)
            pd.append(jnp.exp2((s - m).astype(BF16)))
        for n, (q_ref, qi, h) in enumerate(chains):
            acc = _dot(pd[n], v_ref[qi * tq:(qi + 1) * tq, kc[h]])
            if qi > 0:
                acc = acc + _dot(pp[n], v_ref[0:qi * tq, kc[h]])
            o_ref[qi * tq:(qi + 1) * tq, vc[h]] = (acc[:, 0:MLA_VDIM] / acc[:, MLA_VDIM:]).astype(BF16)

    for step in range((n_tiles + 1) // 2):
        late = n_tiles - 1 - step
        tiles = [(qa_ref, step)] + ([(qb_ref, late)] if late != step else [])
        pl.when(pl.program_id(2) == step)(functools.partial(attend, tiles))


def _flash(q, k, v, batch, seq, tq):
    q3 = q.reshape(batch, seq, MLA_HEADS * MLA_PAD)
    k3 = k.reshape(batch, seq, MLA_HEADS * MLA_PAD)
    v3 = v.reshape(batch, seq, MLA_HEADS * MLA_PAD)
    wq = FLASH_HEADS * MLA_PAD
    wv = FLASH_HEADS * MLA_VDIM
    n_tiles = seq // tq
    out = pl.pallas_call(
        _flash_kernel,
        grid=(batch, MLA_HEADS // FLASH_HEADS, (n_tiles + 1) // 2),
        in_specs=[pl.BlockSpec((None, tq, wq), lambda b, h, i: (b, i, h)),
                  pl.BlockSpec((None, tq, wq), lambda b, h, i: (b, n_tiles - 1 - i, h)),
                  pl.BlockSpec((None, seq, wq), lambda b, h, i: (b, 0, h)),
                  pl.BlockSpec((None, seq, wq), lambda b, h, i: (b, 0, h))],
        out_specs=pl.BlockSpec((None, seq, wv), lambda b, h, i: (b, 0, h)),
        out_shape=jax.ShapeDtypeStruct((batch, seq, MLA_HEADS * MLA_VDIM), BF16),
        compiler_params=_params(("arbitrary", "arbitrary", "arbitrary")),
    )(q3, q3, k3, v3)
    return out.reshape(batch * seq, MLA_HEADS * MLA_VDIM)


def _mla_layouts(w_in, w_uq, q_norm_g, k_norm_g):
    half = MLA_ROPE // 2
    c0 = MLA_Q_LORA + MLA_KV_LORA
    kpe = w_in[:, c0:c0 + MLA_ROPE]
    win = jnp.concatenate([w_in[:, :c0], kpe, kpe, w_in[:, c0 + MLA_ROPE:]], axis=1)
    wq = w_uq.reshape(MLA_Q_LORA, MLA_HEADS, MLA_QK)
    x1 = wq[:, :, MLA_NOPE:MLA_NOPE + half]
    x2 = wq[:, :, MLA_NOPE + half:]
    wuq = jnp.concatenate([wq[:, :, :MLA_NOPE], x1, x2, x2, x1], axis=2).reshape(MLA_Q_LORA, MLA_HEADS * MLA_PAD)
    g1, g2 = q_norm_g[MLA_NOPE:MLA_NOPE + half], q_norm_g[MLA_NOPE + half:]
    qgn = q_norm_g[:MLA_NOPE][None, :]
    qgr = jnp.concatenate([g1, g2, -g2, g1])[None, :]
    kgn = k_norm_g[:MLA_NOPE][None, :]
    kgr = jnp.concatenate([k_norm_g[MLA_NOPE:], k_norm_g[MLA_NOPE:]])[None, :]
    return win, wuq, (qgn, qgr, kgn, kgr)


def kernel(x, mem, positions, mix_norm_g, ffn_norm_g, mem_norm_g, w_mem_kv, mem_q_norm_g, mem_k_norm_g, w_out, w_up, conv_w, conv_b, w_down, hg_w_in, hg_lb_logits, hg_out_norm_g, mla_w_in, mla_qa_norm_g, mla_kva_norm_g, mla_w_uq, mla_w_ukv, mla_q_norm_g, mla_k_norm_g):
    batch, seq, d = x.shape
    n = batch * seq
    x2 = x.reshape(n, d)

    grp = jnp.asarray(np.kron(np.eye(MEM_HEADS), np.full((MEM_HDIM, MEM_HDIM), 1.0 / MEM_HDIM)), BF16)
    sums, tri, lev = _hg_tables()
    hg_tables = (jnp.asarray(sums, BF16), jnp.asarray(tri, BF16), jnp.asarray(lev))

    kmem, vmem = _memkv(mem, mem_norm_g[:, None, :], w_mem_kv.astype(BF16),
                        jnp.tile(mem_k_norm_g, (1, MEM_HEADS))[:, None, :], grp)
    qg = jnp.tile(mem_q_norm_g, (1, MEM_HEADS))[:, None, :]
    w_out_b, w_up_b, w_down_b = w_out.astype(BF16), w_up.astype(BF16), w_down.astype(BF16)

    def finish_layer(xin, y, qm, layer):
        return _layer_tail(xin, y, qm, kmem, vmem, layer, qg, grp, w_out_b, ffn_norm_g[:, None, :],
                           w_up_b, conv_w, conv_b[:, None, :], w_down_b, batch, seq, min(TAIL_ROWS, seq))

    qs, lf, kk, v, sg, qm = _hg_inproj(x2, mix_norm_g[0][None, :], hg_w_in[0].astype(BF16),
                                        hg_lb_logits, min(INPROJ_ROWS, seq))
    y = _hgrn(qs, lf, kk, v, sg, hg_tables, hg_out_norm_g[0][None, :], batch, seq)
    x2 = finish_layer(x2, y, qm, 0)

    win, wuq, tabs = _mla_layouts(mla_w_in[0], mla_w_uq[0], mla_q_norm_g[0], mla_k_norm_g[0])
    q, k, v, qm = _mla_prep(x2, _rope_tables(positions), mix_norm_g[1][None, :], win.astype(BF16),
                            jnp.asarray(_rope_expand(), BF16),
                            mla_qa_norm_g[0][None, :], mla_kva_norm_g[0][None, :], wuq.astype(BF16),
                            mla_w_ukv[0].astype(BF16), tabs, min(PREP_ROWS, seq))
    y = _flash(q, k, v, batch, seq, min(FLASH_TQ, seq))
    x2 = finish_layer(x2, y, qm, 1)
    return x2.reshape(batch, seq, d)
```

```python
import functools

import numpy as np
import jax
import jax.numpy as jnp
from jax import lax
from jax.experimental import pallas as pl
from jax.experimental.pallas import tpu as pltpu

F32 = jnp.float32
BF16 = jnp.bfloat16
EPS = 1e-6

HG_HEADS = 6
HG_KDIM = 128
HG_VDIM = 128
HG_F = HG_HEADS * HG_KDIM
MLA_HEADS = 6
MLA_Q_LORA = 384
MLA_KV_LORA = 256
MLA_NOPE = 128
MLA_ROPE = 64
MLA_QK = MLA_NOPE + MLA_ROPE
MLA_VDIM = 128
MLA_PAD = 256
ROPE_THETA = 10000.0
MEM_HEADS = 4
MEM_HDIM = 64
MEM_W = MEM_HEADS * MEM_HDIM
MIX_W = 768
D_FF = 2816

LANES = 128
SUBLANES = 8
VMEM_LIMIT = 56 * 1024 * 1024

HG_CHUNK = 128
HG_LEVELS = 7
HG_BIG_LEVELS = 4
HG_STEP_CHUNKS = 8
FFN_CHUNK = 256
LOG2E = 1.4426950408889634
FLASH_HEADS = 2
FLASH_TQ = 512
ROPE_ROWS = 512
ROPE_BLOCK = 512
MLA_SUB = 256
ROPE_GROUP = ROPE_BLOCK // (LANES // (MLA_ROPE // 2))
TAIL_ROWS = 1024
PREP_ROWS = 1024
INPROJ_ROWS = 1024


def _dot(a, b):
    return jnp.dot(a, b, preferred_element_type=F32)


def _dot_nt(a, b):
    return lax.dot_general(a, b, (((1,), (1,)), ((), ())), preferred_element_type=F32)


def _dot_tn(a, b):
    return lax.dot_general(a, b, (((0,), (0,)), ((), ())), preferred_element_type=F32)


def _rms(x, g):
    ms = jnp.mean(x * x, axis=-1, keepdims=True)
    return x * lax.rsqrt(ms + EPS) * g


def _sigmoid(x):
    return 1.0 / (1.0 + jnp.exp(-x))


def _split2(x):
    hi = x.astype(BF16)
    lo = (x - hi.astype(F32)).astype(BF16)
    return hi, lo


def _split3(x):
    hi = x.astype(BF16)
    r = x - hi.astype(F32)
    mid = r.astype(BF16)
    lo = (r - mid.astype(F32)).astype(BF16)
    return hi, mid, lo


def _const_spec(shape):
    nd = len(shape)
    return pl.BlockSpec(shape, lambda *_: (0,) * nd, pipeline_mode=pl.Buffered(1))


def _layer_spec(arr, layer):
    nd = arr.ndim - 1
    return pl.BlockSpec((None,) + arr.shape[1:], lambda *_: (layer,) + (0,) * nd,
                        pipeline_mode=pl.Buffered(1))


def _params(sem):
    return pltpu.CompilerParams(dimension_semantics=sem, vmem_limit_bytes=VMEM_LIMIT)


def _hg_inproj_kernel(x_ref, g_ref, w_ref, lbl_ref, qs_ref, lf_ref, kk_ref, v_ref, sg_ref, qm_ref):
    h = _rms(x_ref[...], g_ref[...]).astype(BF16)
    l = lbl_ref[...]
    e = jnp.exp(l - jnp.max(l, axis=0, keepdims=True))
    lb = e[0:1, :] / jnp.sum(e, axis=0, keepdims=True)
    q = _dot(h, w_ref[:, 0:HG_F])
    qs_ref[...] = q * _sigmoid(q) * (HG_KDIM ** -0.5)
    f = _dot(h, w_ref[:, HG_F:2 * HG_F])
    fg = lb + (1.0 - lb) * _sigmoid(f)
    lf_ref[...] = jnp.log2(fg)
    kk_ref[...] = 1.0 - fg
    v_ref[...] = _dot(h, w_ref[:, 2 * HG_F:3 * HG_F]).astype(BF16)
    gate = _dot(h, w_ref[:, 3 * HG_F:4 * HG_F])
    sg_ref[...] = gate * _sigmoid(gate)
    qm_ref[...] = _dot(h, w_ref[:, 4 * HG_F:4 * HG_F + MEM_W])


def _hg_inproj(x2, g, w, lb_logits, tm):
    n, d = x2.shape
    row = lambda i: (i, 0)
    out768 = pl.BlockSpec((tm, HG_F), row)
    return pl.pallas_call(
        _hg_inproj_kernel,
        grid=(n // tm,),
        in_specs=[pl.BlockSpec((tm, d), row), _const_spec((1, d)), _const_spec(w.shape),
                  _const_spec(lb_logits.shape)],
        out_specs=[out768, out768, out768, out768, out768, pl.BlockSpec((tm, MEM_W), row)],
        out_shape=[jax.ShapeDtypeStruct((n, HG_F), F32), jax.ShapeDtypeStruct((n, HG_F), F32),
                   jax.ShapeDtypeStruct((n, HG_F), F32), jax.ShapeDtypeStruct((n, HG_F), BF16),
                   jax.ShapeDtypeStruct((n, HG_F), F32), jax.ShapeDtypeStruct((n, MEM_W), F32)],
        compiler_params=_params(("arbitrary",)),
    )(x2, g, w, lb_logits)


def _memkv_kernel(mem_ref, g_ref, w_ref, kg_ref, grp_ref, k_ref, v_ref):
    mn = _rms(mem_ref[...], g_ref[...]).astype(BF16)
    kv = _dot(mn, w_ref[...])
    k = kv[:, :MEM_W]
    hi, lo = _split2(k * k)
    ms = _dot(hi, grp_ref[...]) + _dot(lo, grp_ref[...])
    k_ref[...] = (k * lax.rsqrt(ms + EPS) * kg_ref[...]).astype(BF16)
    v_ref[...] = kv[:, MEM_W:].astype(BF16)


def _memkv(mem, g, w, kg, grp):
    depth = g.shape[0]
    b, m, d = mem.shape
    spec_kv = pl.BlockSpec((None, None, m, MEM_W), lambda l, i: (l, i, 0, 0))
    return pl.pallas_call(
        _memkv_kernel,
        grid=(depth, b),
        in_specs=[pl.BlockSpec((None, m, d), lambda l, i: (i, 0, 0)),
                  pl.BlockSpec((None, 1, d), lambda l, i: (l, 0, 0)),
                  pl.BlockSpec((None, d, 2 * MEM_W), lambda l, i: (l, 0, 0)),
                  pl.BlockSpec((None, 1, MEM_W), lambda l, i: (l, 0, 0)),
                  _const_spec(grp.shape)],
        out_specs=[spec_kv, spec_kv],
        out_shape=[jax.ShapeDtypeStruct((depth, b, m, MEM_W), BF16)] * 2,
        compiler_params=_params(("arbitrary", "arbitrary")),
    )(mem, g, w, kg, grp)


def _hg_tables():
    C = HG_CHUNK
    t = np.arange(C)[:, None]
    u = np.arange(C)[None, :]
    small = []
    for l in range(HG_BIG_LEVELS, HG_LEVELS):
        c = C >> (l + 1)
        mid = (t // (2 * c)) * (2 * c) + c
        small.append(np.where(t >= mid, (u >= mid) & (u <= t), (u > t) & (u < mid)))
    tri = (u <= t).astype(np.float32)
    sums = np.concatenate([tri] + small, axis=0).astype(np.float32)
    sums = np.concatenate([sums, sums], axis=1)
    p = np.floor(np.log2(np.maximum(t ^ u, 1))).astype(np.int32)
    lev = np.where(t > u, HG_LEVELS - 1 - p, np.where(t == u, HG_LEVELS, -1)).astype(np.int32)
    return sums, tri, lev


def _hgrn_kernel(qs_ref, lf_ref, kk_ref, v_ref, sg_ref, sums_ref, tri_ref, lev_ref, og_ref, y_ref, st_ref):
    C = HG_CHUNK
    heads = range(HG_HEADS)
    cols = [slice(h * HG_KDIM, (h + 1) * HG_KDIM) for h in heads]
    chunks = [slice(s * C, (s + 1) * C) for s in range(qs_ref.shape[0] // C)]

    @pl.when(pl.program_id(1) == 0)
    def _():
        st_ref[...] = jnp.zeros_like(st_ref)

    lev = lev_ref[...]
    masks = [lev == l for l in range(HG_LEVELS + 1)]

    pairs, inter, diag = {}, {}, {}
    for s, rows in enumerate(chunks):
        hi, mid, lo = _split3(lf_ref[rows, :])
        d_all = _dot(sums_ref[...], jnp.concatenate([hi, mid], axis=0))
        b = d_all[0:C, :] + _dot(tri_ref[...], lo)
        d_small = d_all[C:, :]
        for h in heads:
            q = qs_ref[rows, cols[h]]
            k = kk_ref[rows, cols[h]]
            diag[s, h] = jnp.sum(q * k, axis=-1, keepdims=True)
            qb = q.astype(BF16)
            kb = k.astype(BF16)
            bh = b[:, cols[h]]
            ops = []
            for l in range(HG_LEVELS):
                if l < HG_BIG_LEVELS:
                    c = C >> (l + 1)
                    parts = []
                    for lo_row in range(0, C, 2 * c):
                        r = jnp.broadcast_to(bh[lo_row + c - 1:lo_row + c, :], (c, HG_KDIM))
                        parts += [r - bh[lo_row:lo_row + c, :], bh[lo_row + c:lo_row + 2 * c, :] - r]
                    e = jnp.exp2(jnp.concatenate(parts, axis=0))
                else:
                    e = jnp.exp2(d_small[(l - HG_BIG_LEVELS) * C:(l - HG_BIG_LEVELS + 1) * C, cols[h]])
                eb = e.astype(BF16)
                ops.append((qb * eb, kb * eb))
            pairs[s, h] = ops
            e_pre = jnp.exp2(bh)
            e_suf = jnp.exp2(bh[C - 1:C, :] - bh)
            inter[s, h] = (qb * e_pre.astype(BF16), kb * e_suf.astype(BF16), e_pre[C - 1:C, :])

    scores = {}
    for s in range(len(chunks)):
        for h in heads:
            a = jnp.where(masks[HG_LEVELS], diag[s, h], 0.0)
            for l in range(HG_LEVELS):
                a = jnp.where(masks[l], _dot_nt(*pairs[s, h][l]), a)
            scores[s, h] = a.astype(BF16)

    for s, rows in enumerate(chunks):
        for h in heads:
            qe, ke, dec = inter[s, h]
            v = v_ref[rows, cols[h]]
            st = st_ref[h]
            o = _dot_nt(qe, st.astype(BF16)) + _dot(scores[s, h], v)
            st_ref[h] = st * dec + _dot_tn(v, ke)
            y_ref[rows, cols[h]] = (_rms(o, og_ref[...]) * sg_ref[rows, cols[h]]).astype(BF16)


def _hgrn(qs, lf, kk, v, sg, tables, og, batch, seq):
    n = qs.shape[0]
    step = HG_CHUNK * HG_STEP_CHUNKS
    ns = seq // step
    row = lambda b, c: (b * ns + c, 0)
    spec = pl.BlockSpec((step, HG_F), row)
    return pl.pallas_call(
        _hgrn_kernel,
        grid=(batch, ns),
        in_specs=[spec, spec, spec, spec, spec] + [_const_spec(t.shape) for t in tables]
                 + [_const_spec(og.shape)],
        out_specs=spec,
        out_shape=jax.ShapeDtypeStruct((n, HG_F), BF16),
        scratch_shapes=[pltpu.VMEM((HG_HEADS, HG_VDIM, HG_KDIM), F32)],
        compiler_params=_params(("arbitrary", "arbitrary")),
    )(qs, lf, kk, v, sg, *tables, og)


def _mix_out(x, y_ref, qm_ref, km_ref, vm_ref, qg_ref, grp_ref, w_ref):
    q = qm_ref[...]
    hi, lo = _split2(q * q)
    ms = _dot(hi, grp_ref[...]) + _dot(lo, grp_ref[...])
    qn = (q * lax.rsqrt(ms + EPS) * (qg_ref[...] * (MEM_HDIM ** -0.5 * LOG2E))).astype(BF16)
    tm = qn.shape[0]
    head = lax.broadcasted_iota(jnp.int32, qn.shape, 1) // MEM_HDIM
    qs = jnp.concatenate([jnp.where(head == h, qn, jnp.zeros_like(qn)) for h in range(MEM_HEADS)], axis=0)
    s = _dot_nt(qs, km_ref[...])
    p = jnp.exp2(s - jnp.max(s, axis=-1, keepdims=True))
    o = _dot(p.astype(BF16), vm_ref[...]) / jnp.sum(p, axis=-1, keepdims=True)
    m = o[0:tm, :]
    for h in range(1, MEM_HEADS):
        m = jnp.where(head == h, o[h * tm:(h + 1) * tm, :], m)
    return (x + _dot(y_ref[...], w_ref[0:MIX_W, :])
            + _dot(m.astype(BF16), w_ref[MIX_W:MIX_W + MEM_W, :]))


def _shift_row(t, prev):
    r = pltpu.roll(t, 1, axis=0)
    pr = pltpu.roll(prev, 1, axis=0)
    sub = lax.broadcasted_iota(jnp.int32, pr.shape, 0)
    head = jnp.where(sub < 1, pr, r[0:SUBLANES, :])
    return jnp.concatenate([head, r[SUBLANES:, :]], axis=0)


def _conv_ffn(x, g_ref, wup_ref, cw_ref, cb_ref, wdn_ref, carry_ref, act_ref):
    tm = x.shape[0]
    h = _rms(x, g_ref[...]).astype(BF16)

    def conv_slice(c0):
        cols = slice(c0, c0 + FFN_CHUNK)
        u = _dot(h, wup_ref[:, cols])
        prev = carry_ref[:, cols]
        carry_ref[:, cols] = u[tm - SUBLANES:, :]
        w0, w1, w2 = cw_ref[0:1, cols], cw_ref[1:2, cols], cw_ref[2:3, cols]
        t0, p0 = w0 * u, w0 * prev
        t1 = _shift_row(t0, p0) + w1 * u
        p1 = pltpu.roll(p0, 1, axis=0) + w1 * prev
        return _shift_row(t1, p1) + (w2 * u + cb_ref[:, cols])

    for c in range(D_FF // FFN_CHUNK):
        a = conv_slice(c * FFN_CHUNK).astype(BF16)
        b = conv_slice(D_FF + c * FFN_CHUNK).astype(BF16)
        sig = 1.0 / (1.0 + jnp.exp2(a * (-LOG2E)))
        act_ref[:, c * FFN_CHUNK:(c + 1) * FFN_CHUNK] = a * b * sig
    return x + _dot(act_ref[...], wdn_ref[...])


def _tail_kernel(x_ref, y_ref, qm_ref, km_ref, vm_ref, qg_ref, grp_ref, wout_ref,
                 g_ref, wup_ref, cw_ref, cb_ref, wdn_ref, o_ref, carry_ref, act_ref):
    @pl.when(pl.program_id(1) == 0)
    def _():
        carry_ref[...] = jnp.zeros_like(carry_ref)

    x1 = _mix_out(x_ref[...], y_ref, qm_ref, km_ref, vm_ref, qg_ref, grp_ref, wout_ref)
    o_ref[...] = _conv_ffn(x1, g_ref, wup_ref, cw_ref, cb_ref, wdn_ref, carry_ref, act_ref)


def _layer_tail(x2, y, qm, kmem, vmem, layer, qg, grp, wout, g, wup, cw, cb, wdn, batch, seq, tm):
    n, d = x2.shape
    nt = seq // tm
    row = lambda b, t: (b * nt + t, 0)
    mem_spec = pl.BlockSpec((None, None) + kmem.shape[2:], lambda b, t: (layer, b, 0, 0))
    return pl.pallas_call(
        _tail_kernel,
        grid=(batch, nt),
        in_specs=[pl.BlockSpec((tm, d), row), pl.BlockSpec((tm, MIX_W), row),
                  pl.BlockSpec((tm, MEM_W), row), mem_spec, mem_spec,
                  _layer_spec(qg, layer), _const_spec(grp.shape), _layer_spec(wout, layer)]
                 + [_layer_spec(p, layer) for p in (g, wup, cw, cb, wdn)],
        out_specs=pl.BlockSpec((tm, d), row),
        out_shape=jax.ShapeDtypeStruct((n, d), F32),
        scratch_shapes=[pltpu.VMEM((SUBLANES, 2 * D_FF), F32), pltpu.VMEM((tm, D_FF), BF16)],
        compiler_params=_params(("arbitrary", "arbitrary")),
    )(x2, y, qm, kmem, vmem, qg, grp, wout, g, wup, cw, cb, wdn)


def _rope_kernel(pos_ref, inv_ref, cs_ref):
    pf = pos_ref[...].astype(F32)
    grp = lax.broadcasted_iota(jnp.int32, (pf.shape[0], LANES), 1) // (MLA_ROPE // 2)
    p = pf[:, 3:4]
    for a in range(2, -1, -1):
        p = jnp.where(grp == a, pf[:, a:a + 1], p)
    ang = p * inv_ref[...]
    cs_ref[:, 0:LANES] = jnp.cos(ang)
    cs_ref[:, LANES:2 * LANES] = jnp.sin(ang)


def _rope_tables(positions):
    half = MLA_ROPE // 2
    per_row = LANES // half
    n = positions.size
    rows = n // per_row
    tr = min(ROPE_ROWS, rows)
    inv = ROPE_THETA ** (-jnp.arange(half, dtype=F32) / half)
    pos = positions.reshape(n // ROPE_BLOCK, per_row, ROPE_BLOCK // per_row)
    pos = pos.transpose(0, 2, 1).reshape(rows, per_row)
    return pl.pallas_call(
        _rope_kernel,
        grid=(rows // tr,),
        in_specs=[pl.BlockSpec((tr, per_row), lambda i: (i, 0)), _const_spec((1, LANES))],
        out_specs=pl.BlockSpec((tr, 2 * LANES), lambda i: (i, 0)),
        out_shape=jax.ShapeDtypeStruct((rows, 2 * LANES), F32),
        compiler_params=_params(("arbitrary",)),
    )(pos, jnp.tile(inv, per_row)[None, :])


def _rope_expand():
    half = MLA_ROPE // 2
    per_row = LANES // half
    e = np.zeros((per_row, 2 * LANES, 3 * LANES), np.float32)
    j = np.arange(half)
    for a in range(per_row):
        cos_rows = a * half + j
        sin_rows = LANES + a * half + j
        for blk in range(2):
            e[a, cos_rows, blk * half + j] = 1.0
            e[a, sin_rows, (2 + blk) * half + j] = 1.0
        for blk in range(4):
            e[a, cos_rows, LANES + blk * half + j] = 1.0
            e[a, sin_rows, 2 * LANES + blk * half + j] = -1.0 if blk % 2 == 0 else 1.0
    return e


def _mla_prep_kernel(x_ref, cs_ref, g_ref, win_ref, exp_ref, qag_ref, kvag_ref, wuq_ref, wukv_ref,
                     qgn_ref, qgr_ref, kgn_ref, kgr_ref,
                     q_ref, k_ref, v_ref, qm_ref):
    subs = [slice(r * MLA_SUB, (r + 1) * MLA_SUB) for r in range(x_ref.shape[0] // MLA_SUB)]
    c0 = MLA_Q_LORA + MLA_KV_LORA
    zs = [_dot(_rms(x_ref[rows, :], g_ref[...]).astype(BF16), win_ref[...]) for rows in subs]
    qfs = [_dot(_rms(z[:, 0:MLA_Q_LORA], qag_ref[...]).astype(BF16), wuq_ref[...]) for z in zs]
    kvfs = [_dot(_rms(z[:, MLA_Q_LORA:c0], kvag_ref[...]).astype(BF16), wukv_ref[...]) for z in zs]

    for rows, z, qf, kvf in zip(subs, zs, qfs, kvfs):
        kpe2 = z[:, c0:c0 + LANES]
        qm_ref[rows, :] = z[:, c0 + LANES:]

        blk, part = divmod(rows.start, ROPE_BLOCK)
        drows = slice(blk * ROPE_GROUP, (blk + 1) * ROPE_GROUP)
        hi, lo = _split2(cs_ref[drows, :])
        rope = jnp.concatenate(
            [_dot(hi, exp_ref[a]) + _dot(lo, exp_ref[a])
             for a in range(part // ROPE_GROUP, (part + MLA_SUB) // ROPE_GROUP)], axis=0)
        tq = rope[:, 0:LANES] * qgr_ref[...]
        cos = rope[:, LANES:2 * LANES]
        sin_signed = rope[:, 2 * LANES:3 * LANES]

        for hd in range(MLA_HEADS):
            qn = qf[:, hd * MLA_PAD:hd * MLA_PAD + MLA_NOPE]
            qr = qf[:, hd * MLA_PAD + MLA_NOPE:(hd + 1) * MLA_PAD]
            ss = jnp.sum(qn * qn, axis=-1, keepdims=True) + 0.5 * jnp.sum(qr * qr, axis=-1, keepdims=True)
            rs = lax.rsqrt(ss * (1.0 / MLA_QK) + EPS) * (MLA_QK ** -0.5 * LOG2E)
            q_ref[rows, hd * MLA_PAD:hd * MLA_PAD + MLA_NOPE] = (qn * rs * qgn_ref[...]).astype(BF16)
            q_ref[rows, hd * MLA_PAD + MLA_NOPE:(hd + 1) * MLA_PAD] = (qr * rs * tq).astype(BF16)

        pk = kpe2 * kgr_ref[...]
        kr = pk * cos + pltpu.roll(pk, MLA_ROPE // 2, axis=1) * sin_signed
        sk = 0.5 * jnp.sum(kpe2 * kpe2, axis=-1, keepdims=True)
        for hd in range(MLA_HEADS):
            kn = kvf[:, hd * MLA_PAD:hd * MLA_PAD + MLA_NOPE]
            ss = jnp.sum(kn * kn, axis=-1, keepdims=True) + sk
            rs = lax.rsqrt(ss * (1.0 / MLA_QK) + EPS)
            k_ref[rows, hd * MLA_PAD:hd * MLA_PAD + MLA_NOPE] = (kn * rs * kgn_ref[...]).astype(BF16)
            k_ref[rows, hd * MLA_PAD + MLA_NOPE:(hd + 1) * MLA_PAD] = (kr * rs).astype(BF16)
            v_ref[rows, hd * MLA_PAD:hd * MLA_PAD + MLA_VDIM] = kvf[:, hd * MLA_PAD + MLA_NOPE:(hd + 1) * MLA_PAD].astype(BF16)
            v_ref[rows, hd * MLA_PAD + MLA_VDIM:(hd + 1) * MLA_PAD] = jnp.ones((MLA_SUB, MLA_PAD - MLA_VDIM), BF16)


def _mla_prep(x2, cs, g, win, expand, qag, kvag, wuq, wukv, tabs, tm):
    n, d = x2.shape
    row = lambda i: (i, 0)
    consts = [g, win, expand, qag, kvag, wuq, wukv] + list(tabs)
    return pl.pallas_call(
        _mla_prep_kernel,
        grid=(n // tm,),
        in_specs=[pl.BlockSpec((tm, d), row), pl.BlockSpec((tm // ROPE_BLOCK * ROPE_GROUP, 2 * LANES), row)]
                 + [_const_spec(c.shape) for c in consts],
        out_specs=[pl.BlockSpec((tm, MLA_HEADS * MLA_PAD), row), pl.BlockSpec((tm, MLA_HEADS * MLA_PAD), row),
                   pl.BlockSpec((tm, MLA_HEADS * MLA_PAD), row), pl.BlockSpec((tm, MEM_W), row)],
        out_shape=[jax.ShapeDtypeStruct((n, MLA_HEADS * MLA_PAD), BF16),
                   jax.ShapeDtypeStruct((n, MLA_HEADS * MLA_PAD), BF16),
                   jax.ShapeDtypeStruct((n, MLA_HEADS * MLA_PAD), BF16),
                   jax.ShapeDtypeStruct((n, MEM_W), F32)],
        compiler_params=_params(("arbitrary",)),
    )(x2, cs, *consts)


def _flash_kernel(qa_ref, qb_ref, k_ref, v_ref, o_ref):
    tq = qa_ref.shape[0]
    n_tiles = k_ref.shape[0] // tq
    r = lax.broadcasted_iota(jnp.int32, (tq, tq), 0)
    c = lax.broadcasted_iota(jnp.int32, (tq, tq), 1)
    heads = range(FLASH_HEADS)
    kc = [slice(h * MLA_PAD, (h + 1) * MLA_PAD) for h in heads]
    vc = [slice(h * MLA_VDIM, (h + 1) * MLA_VDIM) for h in heads]

    def attend(tiles):
        chains = [(q_ref, qi, h) for q_ref, qi in tiles for h in heads]
        sd = [_dot_nt(q_ref[:, kc[h]], k_ref[qi * tq:(qi + 1) * tq, kc[h]]) for q_ref, qi, h in chains]
        sp = [_dot_nt(q_ref[:, kc[h]], k_ref[0:qi * tq, kc[h]]) if qi > 0 else None for q_ref, qi, h in chains]
        pd, pp = [], []
        for n, (q_ref, qi, h) in enumerate(chains):
            s = jnp.where(c <= r, sd[n], -jnp.inf)
            m = jnp.max(s, axis=-1, keepdims=True)
            if qi > 0:
                m = jnp.maximum(m, jnp.max(sp[n], axis=-1, keepdims=True))
                pp.append(jnp.exp2((sp[n] - m).astype(BF16)))
            else:
                pp.append(None)
            pd.append(jnp.exp2((s - m).astype(BF16)))
        for n, (q_ref, qi, h) in enumerate(chains):
            acc = _dot(pd[n], v_ref[qi * tq:(qi + 1) * tq, kc[h]])
            if qi > 0:
                acc = acc + _dot(pp[n], v_ref[0:qi * tq, kc[h]])
            o_ref[qi * tq:(qi + 1) * tq, vc[h]] = (acc[:, 0:MLA_VDIM] / acc[:, MLA_VDIM:]).astype(BF16)

    for step in range((n_tiles + 1) // 2):
        late = n_tiles - 1 - step
        tiles = [(qa_ref, step)] + ([(qb_ref, late)] if late != step else [])
        pl.when(pl.program_id(2) == step)(functools.partial(attend, tiles))


def _flash(q, k, v, batch, seq, tq):
    q3 = q.reshape(batch, seq, MLA_HEADS * MLA_PAD)
    k3 = k.reshape(batch, seq, MLA_HEADS * MLA_PAD)
    v3 = v.reshape(batch, seq, MLA_HEADS * MLA_PAD)
    wq = FLASH_HEADS * MLA_PAD
    wv = FLASH_HEADS * MLA_VDIM
    n_tiles = seq // tq
    out = pl.pallas_call(
        _flash_kernel,
        grid=(batch, MLA_HEADS // FLASH_HEADS, (n_tiles + 1) // 2),
        in_specs=[pl.BlockSpec((None, tq, wq), lambda b, h, i: (b, i, h)),
                  pl.BlockSpec((None, tq, wq), lambda b, h, i: (b, n_tiles - 1 - i, h)),
                  pl.BlockSpec((None, seq, wq), lambda b, h, i: (b, 0, h)),
                  pl.BlockSpec((None, seq, wq), lambda b, h, i: (b, 0, h))],
        out_specs=pl.BlockSpec((None, seq, wv), lambda b, h, i: (b, 0, h)),
        out_shape=jax.ShapeDtypeStruct((batch, seq, MLA_HEADS * MLA_VDIM), BF16),
        compiler_params=_params(("arbitrary", "arbitrary", "arbitrary")),
    )(q3, q3, k3, v3)
    return out.reshape(batch * seq, MLA_HEADS * MLA_VDIM)


def _mla_layouts(w_in, w_uq, q_norm_g, k_norm_g):
    half = MLA_ROPE // 2
    c0 = MLA_Q_LORA + MLA_KV_LORA
    kpe = w_in[:, c0:c0 + MLA_ROPE]
    win = jnp.concatenate([w_in[:, :c0], kpe, kpe, w_in[:, c0 + MLA_ROPE:]], axis=1)
    wq = w_uq.reshape(MLA_Q_LORA, MLA_HEADS, MLA_QK)
    x1 = wq[:, :, MLA_NOPE:MLA_NOPE + half]
    x2 = wq[:, :, MLA_NOPE + half:]
    wuq = jnp.concatenate([wq[:, :, :MLA_NOPE], x1, x2, x2, x1], axis=2).reshape(MLA_Q_LORA, MLA_HEADS * MLA_PAD)
    g1, g2 = q_norm_g[MLA_NOPE:MLA_NOPE + half], q_norm_g[MLA_NOPE + half:]
    qgn = q_norm_g[:MLA_NOPE][None, :]
    qgr = jnp.concatenate([g1, g2, -g2, g1])[None, :]
    kgn = k_norm_g[:MLA_NOPE][None, :]
    kgr = jnp.concatenate([k_norm_g[MLA_NOPE:], k_norm_g[MLA_NOPE:]])[None, :]
    return win, wuq, (qgn, qgr, kgn, kgr)


def kernel(x, mem, positions, mix_norm_g, ffn_norm_g, mem_norm_g, w_mem_kv, mem_q_norm_g, mem_k_norm_g, w_out, w_up, conv_w, conv_b, w_down, hg_w_in, hg_lb_logits, hg_out_norm_g, mla_w_in, mla_qa_norm_g, mla_kva_norm_g, mla_w_uq, mla_w_ukv, mla_q_norm_g, mla_k_norm_g):
    batch, seq, d = x.shape
    n = batch * seq
    x2 = x.reshape(n, d)

    grp = jnp.asarray(np.kron(np.eye(MEM_HEADS), np.full((MEM_HDIM, MEM_HDIM), 1.0 / MEM_HDIM)), BF16)
    sums, tri, lev = _hg_tables()
    hg_tables = (jnp.asarray(sums, BF16), jnp.asarray(tri, BF16), jnp.asarray(lev))

    kmem, vmem = _memkv(mem, mem_norm_g[:, None, :], w_mem_kv.astype(BF16),
                        jnp.tile(mem_k_norm_g, (1, MEM_HEADS))[:, None, :], grp)
    qg = jnp.tile(mem_q_norm_g, (1, MEM_HEADS))[:, None, :]
    w_out_b, w_up_b, w_down_b = w_out.astype(BF16), w_up.astype(BF16), w_down.astype(BF16)

    def finish_layer(xin, y, qm, layer):
        return _layer_tail(xin, y, qm, kmem, vmem, layer, qg, grp, w_out_b, ffn_norm_g[:, None, :],
                           w_up_b, conv_w, conv_b[:, None, :], w_down_b, batch, seq, min(TAIL_ROWS, seq))

    qs, lf, kk, v, sg, qm = _hg_inproj(x2, mix_norm_g[0][None, :], hg_w_in[0].astype(BF16),
                                        hg_lb_logits, min(INPROJ_ROWS, seq))
    y = _hgrn(qs, lf, kk, v, sg, hg_tables, hg_out_norm_g[0][None, :], batch, seq)
    x2 = finish_layer(x2, y, qm, 0)

    win, wuq, tabs = _mla_layouts(mla_w_in[0], mla_w_uq[0], mla_q_norm_g[0], mla_k_norm_g[0])
    q, k, v, qm = _mla_prep(x2, _rope_tables(positions), mix_norm_g[1][None, :], win.astype(BF16),
                            jnp.asarray(_rope_expand(), BF16),
                            mla_qa_norm_g[0][None, :], mla_kva_norm_g[0][None, :], wuq.astype(BF16),
                            mla_w_ukv[0].astype(BF16), tabs, min(PREP_ROWS, seq))
    y = _flash(q, k, v, batch, seq, min(FLASH_TQ, seq))
    x2 = finish_layer(x2, y, qm, 1)
    return x2.reshape(batch, seq, d)
```

```python
import functools

import numpy as np
import jax
import jax.numpy as jnp
from jax import lax
from jax.experimental import pallas as pl
from jax.experimental.pallas import tpu as pltpu

F32 = jnp.float32
BF16 = jnp.bfloat16
EPS = 1e-6

HG_HEADS = 6
HG_KDIM = 128
HG_VDIM = 128
HG_F = HG_HEADS * HG_KDIM
MLA_HEADS = 6
MLA_Q_LORA = 384
MLA_KV_LORA = 256
MLA_NOPE = 128
MLA_ROPE = 64
MLA_QK = MLA_NOPE + MLA_ROPE
MLA_VDIM = 128
MLA_PAD = 256
ROPE_THETA = 10000.0
MEM_HEADS = 4
MEM_HDIM = 64
MEM_W = MEM_HEADS * MEM_HDIM
MIX_W = 768
D_FF = 2816

LANES = 128
SUBLANES = 8
VMEM_LIMIT = 56 * 1024 * 1024

HG_CHUNK = 128
HG_LEVELS = 7
HG_BIG_LEVELS = 4
HG_STEP_CHUNKS = 8
FFN_CHUNK = 256
LOG2E = 1.4426950408889634
FLASH_HEADS = 2
FLASH_TQ = 512
ROPE_ROWS = 512
ROPE_BLOCK = 512
MLA_SUB = 256
ROPE_GROUP = ROPE_BLOCK // (LANES // (MLA_ROPE // 2))
TAIL_ROWS = 1024
PREP_ROWS = 1024
HG_PROJ_ROWS = 512


def _dot(a, b):
    return jnp.dot(a, b, preferred_element_type=F32)


def _dot_nt(a, b):
    return lax.dot_general(a, b, (((1,), (1,)), ((), ())), preferred_element_type=F32)


def _dot_tn(a, b):
    return lax.dot_general(a, b, (((0,), (0,)), ((), ())), preferred_element_type=F32)


def _rms(x, g):
    ms = jnp.mean(x * x, axis=-1, keepdims=True)
    return x * lax.rsqrt(ms + EPS) * g


def _sigmoid(x):
    return 1.0 / (1.0 + jnp.exp(-x))


def _split2(x):
    hi = x.astype(BF16)
    lo = (x - hi.astype(F32)).astype(BF16)
    return hi, lo


def _split3(x):
    hi = x.astype(BF16)
    r = x - hi.astype(F32)
    mid = r.astype(BF16)
    lo = (r - mid.astype(F32)).astype(BF16)
    return hi, mid, lo


def _const_spec(shape):
    nd = len(shape)
    return pl.BlockSpec(shape, lambda *_: (0,) * nd, pipeline_mode=pl.Buffered(1))


def _layer_spec(arr, layer):
    nd = arr.ndim - 1
    return pl.BlockSpec((None,) + arr.shape[1:], lambda *_: (layer,) + (0,) * nd,
                        pipeline_mode=pl.Buffered(1))


def _params(sem):
    return pltpu.CompilerParams(dimension_semantics=sem, vmem_limit_bytes=VMEM_LIMIT)


def _memkv_kernel(mem_ref, g_ref, w_ref, kg_ref, grp_ref, k_ref, v_ref):
    mn = _rms(mem_ref[...], g_ref[...]).astype(BF16)
    kv = _dot(mn, w_ref[...])
    k = kv[:, :MEM_W]
    hi, lo = _split2(k * k)
    ms = _dot(hi, grp_ref[...]) + _dot(lo, grp_ref[...])
    k_ref[...] = (k * lax.rsqrt(ms + EPS) * kg_ref[...]).astype(BF16)
    v_ref[...] = kv[:, MEM_W:].astype(BF16)


def _memkv(mem, g, w, kg, grp):
    depth = g.shape[0]
    b, m, d = mem.shape
    spec_kv = pl.BlockSpec((None, None, m, MEM_W), lambda l, i: (l, i, 0, 0))
    return pl.pallas_call(
        _memkv_kernel,
        grid=(depth, b),
        in_specs=[pl.BlockSpec((None, m, d), lambda l, i: (i, 0, 0)),
                  pl.BlockSpec((None, 1, d), lambda l, i: (l, 0, 0)),
                  pl.BlockSpec((None, d, 2 * MEM_W), lambda l, i: (l, 0, 0)),
                  pl.BlockSpec((None, 1, MEM_W), lambda l, i: (l, 0, 0)),
                  _const_spec(grp.shape)],
        out_specs=[spec_kv, spec_kv],
        out_shape=[jax.ShapeDtypeStruct((depth, b, m, MEM_W), BF16)] * 2,
        compiler_params=_params(("arbitrary", "arbitrary")),
    )(mem, g, w, kg, grp)


def _hg_tables():
    C = HG_CHUNK
    t = np.arange(C)[:, None]
    u = np.arange(C)[None, :]
    small = []
    for l in range(HG_BIG_LEVELS, HG_LEVELS):
        c = C >> (l + 1)
        mid = (t // (2 * c)) * (2 * c) + c
        small.append(np.where(t >= mid, (u >= mid) & (u <= t), (u > t) & (u < mid)))
    tri = (u <= t).astype(np.float32)
    sums = np.concatenate([tri] + small, axis=0).astype(np.float32)
    sums = np.concatenate([sums, sums], axis=1)
    p = np.floor(np.log2(np.maximum(t ^ u, 1))).astype(np.int32)
    lev = np.where(t > u, HG_LEVELS - 1 - p, np.where(t == u, HG_LEVELS, -1)).astype(np.int32)
    return sums, tri, lev


def _hg_layer_kernel(x_ref, g_ref, w_ref, lbl_ref, sums_ref, tri_ref, lev_ref, og_ref, y_ref, qm_ref, st_ref):
    C = HG_CHUNK
    heads = range(HG_HEADS)
    cols = [slice(h * HG_KDIM, (h + 1) * HG_KDIM) for h in heads]
    n_rows = x_ref.shape[0]

    @pl.when(pl.program_id(1) == 0)
    def _():
        st_ref[...] = jnp.zeros_like(st_ref)

    l = lbl_ref[...]
    e = jnp.exp(l - jnp.max(l, axis=0, keepdims=True))
    lb = e[0:1, :] / jnp.sum(e, axis=0, keepdims=True)

    proj = []
    for r0 in range(0, n_rows, HG_PROJ_ROWS):
        rows = slice(r0, r0 + HG_PROJ_ROWS)
        h = _rms(x_ref[rows, :], g_ref[...]).astype(BF16)
        f = _dot(h, w_ref[:, HG_F:2 * HG_F])
        fg = lb + (1.0 - lb) * _sigmoid(f)
        q = _dot(h, w_ref[:, 0:HG_F])
        gate = _dot(h, w_ref[:, 3 * HG_F:4 * HG_F])
        proj.append((q * _sigmoid(q) * (HG_KDIM ** -0.5), jnp.log2(fg), 1.0 - fg,
                     _dot(h, w_ref[:, 2 * HG_F:3 * HG_F]).astype(BF16), gate * _sigmoid(gate)))
        qm_ref[rows, :] = _dot(h, w_ref[:, 4 * HG_F:4 * HG_F + MEM_W])

    def chunk_of(which, s):
        tile, off = divmod(s * C, HG_PROJ_ROWS)
        return proj[tile][which][off:off + C, :]

    n_chunks = n_rows // C
    lev = lev_ref[...]
    masks = [lev == l for l in range(HG_LEVELS + 1)]

    pairs, inter, diag = {}, {}, {}
    for s in range(n_chunks):
        qs, kk = chunk_of(0, s), chunk_of(2, s)
        hi, mid, lo = _split3(chunk_of(1, s))
        d_all = _dot(sums_ref[...], jnp.concatenate([hi, mid], axis=0))
        b = d_all[0:C, :] + _dot(tri_ref[...], lo)
        d_small = d_all[C:, :]
        for h in heads:
            q = qs[:, cols[h]]
            k = kk[:, cols[h]]
            diag[s, h] = jnp.sum(q * k, axis=-1, keepdims=True)
            qb = q.astype(BF16)
            kb = k.astype(BF16)
            bh = b[:, cols[h]]
            ops = []
            for l in range(HG_LEVELS):
                if l < HG_BIG_LEVELS:
                    c = C >> (l + 1)
                    parts = []
                    for lo_row in range(0, C, 2 * c):
                        r = jnp.broadcast_to(bh[lo_row + c - 1:lo_row + c, :], (c, HG_KDIM))
                        parts += [r - bh[lo_row:lo_row + c, :], bh[lo_row + c:lo_row + 2 * c, :] - r]
                    e = jnp.exp2(jnp.concatenate(parts, axis=0))
                else:
                    e = jnp.exp2(d_small[(l - HG_BIG_LEVELS) * C:(l - HG_BIG_LEVELS + 1) * C, cols[h]])
                eb = e.astype(BF16)
                ops.append((qb * eb, kb * eb))
            pairs[s, h] = ops
            e_pre = jnp.exp2(bh)
            e_suf = jnp.exp2(bh[C - 1:C, :] - bh)
            inter[s, h] = (qb * e_pre.astype(BF16), kb * e_suf.astype(BF16), e_pre[C - 1:C, :])

    scores = {}
    for s in range(n_chunks):
        for h in heads:
            a = jnp.where(masks[HG_LEVELS], diag[s, h], 0.0)
            for l in range(HG_LEVELS):
                a = jnp.where(masks[l], _dot_nt(*pairs[s, h][l]), a)
            scores[s, h] = a.astype(BF16)

    for s in range(n_chunks):
        rows = slice(s * C, (s + 1) * C)
        vv, sg = chunk_of(3, s), chunk_of(4, s)
        for h in heads:
            qe, ke, dec = inter[s, h]
            v = vv[:, cols[h]]
            st = st_ref[h]
            o = _dot_nt(qe, st.astype(BF16)) + _dot(scores[s, h], v)
            st_ref[h] = st * dec + _dot_tn(v, ke)
            y_ref[rows, cols[h]] = (_rms(o, og_ref[...]) * sg[:, cols[h]]).astype(BF16)


def _hg_layer(x2, g, w, lb_logits, tables, og, batch, seq):
    n, d = x2.shape
    step = min(HG_CHUNK * HG_STEP_CHUNKS, seq)
    ns = seq // step
    row = lambda b, c: (b * ns + c, 0)
    return pl.pallas_call(
        _hg_layer_kernel,
        grid=(batch, ns),
        in_specs=[pl.BlockSpec((step, d), row), _const_spec(g.shape), _const_spec(w.shape),
                  _const_spec(lb_logits.shape)] + [_const_spec(t.shape) for t in tables]
                 + [_const_spec(og.shape)],
        out_specs=[pl.BlockSpec((step, HG_F), row), pl.BlockSpec((step, MEM_W), row)],
        out_shape=[jax.ShapeDtypeStruct((n, HG_F), BF16), jax.ShapeDtypeStruct((n, MEM_W), F32)],
        scratch_shapes=[pltpu.VMEM((HG_HEADS, HG_VDIM, HG_KDIM), F32)],
        compiler_params=_params(("arbitrary", "arbitrary")),
    )(x2, g, w, lb_logits, *tables, og)


def _mix_out(x, y_ref, qm_ref, km_ref, vm_ref, qg_ref, grp_ref, w_ref):
    q = qm_ref[...]
    hi, lo = _split2(q * q)
    ms = _dot(hi, grp_ref[...]) + _dot(lo, grp_ref[...])
    qn = (q * lax.rsqrt(ms + EPS) * (qg_ref[...] * (MEM_HDIM ** -0.5 * LOG2E))).astype(BF16)
    tm = qn.shape[0]
    head = lax.broadcasted_iota(jnp.int32, qn.shape, 1) // MEM_HDIM
    qs = jnp.concatenate([jnp.where(head == h, qn, jnp.zeros_like(qn)) for h in range(MEM_HEADS)], axis=0)
    s = _dot_nt(qs, km_ref[...])
    p = jnp.exp2(s - jnp.max(s, axis=-1, keepdims=True))
    o = _dot(p.astype(BF16), vm_ref[...]) / jnp.sum(p, axis=-1, keepdims=True)
    m = o[0:tm, :]
    for h in range(1, MEM_HEADS):
        m = jnp.where(head == h, o[h * tm:(h + 1) * tm, :], m)
    return (x + _dot(y_ref[...], w_ref[0:MIX_W, :])
            + _dot(m.astype(BF16), w_ref[MIX_W:MIX_W + MEM_W, :]))


def _shift_row(t, prev):
    r = pltpu.roll(t, 1, axis=0)
    pr = pltpu.roll(prev, 1, axis=0)
    sub = lax.broadcasted_iota(jnp.int32, pr.shape, 0)
    head = jnp.where(sub < 1, pr, r[0:SUBLANES, :])
    return jnp.concatenate([head, r[SUBLANES:, :]], axis=0)


def _conv_ffn(x, g_ref, wup_ref, cw_ref, cb_ref, wdn_ref, carry_ref, act_ref):
    tm = x.shape[0]
    h = _rms(x, g_ref[...]).astype(BF16)

    def conv_slice(c0):
        cols = slice(c0, c0 + FFN_CHUNK)
        u = _dot(h, wup_ref[:, cols])
        prev = carry_ref[:, cols]
        carry_ref[:, cols] = u[tm - SUBLANES:, :]
        w0, w1, w2 = cw_ref[0:1, cols], cw_ref[1:2, cols], cw_ref[2:3, cols]
        t0, p0 = w0 * u, w0 * prev
        t1 = _shift_row(t0, p0) + w1 * u
        p1 = pltpu.roll(p0, 1, axis=0) + w1 * prev
        return _shift_row(t1, p1) + (w2 * u + cb_ref[:, cols])

    for c in range(D_FF // FFN_CHUNK):
        a = conv_slice(c * FFN_CHUNK).astype(BF16)
        b = conv_slice(D_FF + c * FFN_CHUNK).astype(BF16)
        sig = 1.0 / (1.0 + jnp.exp2(a * (-LOG2E)))
        act_ref[:, c * FFN_CHUNK:(c + 1) * FFN_CHUNK] = a * b * sig
    return x + _dot(act_ref[...], wdn_ref[...])


def _tail_kernel(x_ref, y_ref, qm_ref, km_ref, vm_ref, qg_ref, grp_ref, wout_ref,
                 g_ref, wup_ref, cw_ref, cb_ref, wdn_ref, o_ref, carry_ref, act_ref):
    @pl.when(pl.program_id(1) == 0)
    def _():
        carry_ref[...] = jnp.zeros_like(carry_ref)

    x1 = _mix_out(x_ref[...], y_ref, qm_ref, km_ref, vm_ref, qg_ref, grp_ref, wout_ref)
    o_ref[...] = _conv_ffn(x1, g_ref, wup_ref, cw_ref, cb_ref, wdn_ref, carry_ref, act_ref)


def _layer_tail(x2, y, qm, kmem, vmem, layer, qg, grp, wout, g, wup, cw, cb, wdn, batch, seq, tm):
    n, d = x2.shape
    nt = seq // tm
    row = lambda b, t: (b * nt + t, 0)
    mem_spec = pl.BlockSpec((None, None) + kmem.shape[2:], lambda b, t: (layer, b, 0, 0))
    return pl.pallas_call(
        _tail_kernel,
        grid=(batch, nt),
        in_specs=[pl.BlockSpec((tm, d), row), pl.BlockSpec((tm, MIX_W), row),
                  pl.BlockSpec((tm, MEM_W), row), mem_spec, mem_spec,
                  _layer_spec(qg, layer), _const_spec(grp.shape), _layer_spec(wout, layer)]
                 + [_layer_spec(p, layer) for p in (g, wup, cw, cb, wdn)],
        out_specs=pl.BlockSpec((tm, d), row),
        out_shape=jax.ShapeDtypeStruct((n, d), F32),
        scratch_shapes=[pltpu.VMEM((SUBLANES, 2 * D_FF), F32), pltpu.VMEM((tm, D_FF), BF16)],
        compiler_params=_params(("arbitrary", "arbitrary")),
    )(x2, y, qm, kmem, vmem, qg, grp, wout, g, wup, cw, cb, wdn)


def _rope_kernel(pos_ref, inv_ref, cs_ref):
    pf = pos_ref[...].astype(F32)
    grp = lax.broadcasted_iota(jnp.int32, (pf.shape[0], LANES), 1) // (MLA_ROPE // 2)
    p = pf[:, 3:4]
    for a in range(2, -1, -1):
        p = jnp.where(grp == a, pf[:, a:a + 1], p)
    ang = p * inv_ref[...]
    cs_ref[:, 0:LANES] = jnp.cos(ang)
    cs_ref[:, LANES:2 * LANES] = jnp.sin(ang)


def _rope_tables(positions):
    half = MLA_ROPE // 2
    per_row = LANES // half
    n = positions.size
    rows = n // per_row
    tr = min(ROPE_ROWS, rows)
    inv = ROPE_THETA ** (-jnp.arange(half, dtype=F32) / half)
    pos = positions.reshape(n // ROPE_BLOCK, per_row, ROPE_BLOCK // per_row)
    pos = pos.transpose(0, 2, 1).reshape(rows, per_row)
    return pl.pallas_call(
        _rope_kernel,
        grid=(rows // tr,),
        in_specs=[pl.BlockSpec((tr, per_row), lambda i: (i, 0)), _const_spec((1, LANES))],
        out_specs=pl.BlockSpec((tr, 2 * LANES), lambda i: (i, 0)),
        out_shape=jax.ShapeDtypeStruct((rows, 2 * LANES), F32),
        compiler_params=_params(("arbitrary",)),
    )(pos, jnp.tile(inv, per_row)[None, :])


def _rope_expand():
    half = MLA_ROPE // 2
    per_row = LANES // half
    e = np.zeros((per_row, 2 * LANES, 3 * LANES), np.float32)
    j = np.arange(half)
    for a in range(per_row):
        cos_rows = a * half + j
        sin_rows = LANES + a * half + j
        for blk in range(2):
            e[a, cos_rows, blk * half + j] = 1.0
            e[a, sin_rows, (2 + blk) * half + j] = 1.0
        for blk in range(4):
            e[a, cos_rows, LANES + blk * half + j] = 1.0
            e[a, sin_rows, 2 * LANES + blk * half + j] = -1.0 if blk % 2 == 0 else 1.0
    return e


def _mla_prep_kernel(x_ref, cs_ref, g_ref, win_ref, exp_ref, qag_ref, kvag_ref, wuq_ref, wukv_ref,
                     qgn_ref, qgr_ref, kgn_ref, kgr_ref,
                     q_ref, k_ref, v_ref, qm_ref):
    subs = [slice(r * MLA_SUB, (r + 1) * MLA_SUB) for r in range(x_ref.shape[0] // MLA_SUB)]
    c0 = MLA_Q_LORA + MLA_KV_LORA
    zs = [_dot(_rms(x_ref[rows, :], g_ref[...]).astype(BF16), win_ref[...]) for rows in subs]
    qfs = [_dot(_rms(z[:, 0:MLA_Q_LORA], qag_ref[...]).astype(BF16), wuq_ref[...]) for z in zs]
    kvfs = [_dot(_rms(z[:, MLA_Q_LORA:c0], kvag_ref[...]).astype(BF16), wukv_ref[...]) for z in zs]

    for rows, z, qf, kvf in zip(subs, zs, qfs, kvfs):
        kpe2 = z[:, c0:c0 + LANES]
        qm_ref[rows, :] = z[:, c0 + LANES:]

        blk, part = divmod(rows.start, ROPE_BLOCK)
        drows = slice(blk * ROPE_GROUP, (blk + 1) * ROPE_GROUP)
        hi, lo = _split2(cs_ref[drows, :])
        rope = jnp.concatenate(
            [_dot(hi, exp_ref[a]) + _dot(lo, exp_ref[a])
             for a in range(part // ROPE_GROUP, (part + MLA_SUB) // ROPE_GROUP)], axis=0)
        tq = rope[:, 0:LANES] * qgr_ref[...]
        cos = rope[:, LANES:2 * LANES]
        sin_signed = rope[:, 2 * LANES:3 * LANES]

        for hd in range(MLA_HEADS):
            qn = qf[:, hd * MLA_PAD:hd * MLA_PAD + MLA_NOPE]
            qr = qf[:, hd * MLA_PAD + MLA_NOPE:(hd + 1) * MLA_PAD]
            ss = jnp.sum(qn * qn, axis=-1, keepdims=True) + 0.5 * jnp.sum(qr * qr, axis=-1, keepdims=True)
            rs = lax.rsqrt(ss * (1.0 / MLA_QK) + EPS) * (MLA_QK ** -0.5 * LOG2E)
            q_ref[rows, hd * MLA_PAD:hd * MLA_PAD + MLA_NOPE] = (qn * rs * qgn_ref[...]).astype(BF16)
            q_ref[rows, hd * MLA_PAD + MLA_NOPE:(hd + 1) * MLA_PAD] = (qr * rs * tq).astype(BF16)

        pk = kpe2 * kgr_ref[...]
        kr = pk * cos + pltpu.roll(pk, MLA_ROPE // 2, axis=1) * sin_signed
        sk = 0.5 * jnp.sum(kpe2 * kpe2, axis=-1, keepdims=True)
        for hd in range(MLA_HEADS):
            kn = kvf[:, hd * MLA_PAD:hd * MLA_PAD + MLA_NOPE]
            ss = jnp.sum(kn * kn, axis=-1, keepdims=True) + sk
            rs = lax.rsqrt(ss * (1.0 / MLA_QK) + EPS)
            k_ref[rows, hd * MLA_PAD:hd * MLA_PAD + MLA_NOPE] = (kn * rs * kgn_ref[...]).astype(BF16)
            k_ref[rows, hd * MLA_PAD + MLA_NOPE:(hd + 1) * MLA_PAD] = (kr * rs).astype(BF16)
            v_ref[rows, hd * MLA_PAD:hd * MLA_PAD + MLA_VDIM] = kvf[:, hd * MLA_PAD + MLA_NOPE:(hd + 1) * MLA_PAD].astype(BF16)
            v_ref[rows, hd * MLA_PAD + MLA_VDIM:(hd + 1) * MLA_PAD] = jnp.ones((MLA_SUB, MLA_PAD - MLA_VDIM), BF16)


def _mla_prep(x2, cs, g, win, expand, qag, kvag, wuq, wukv, tabs, tm):
    n, d = x2.shape
    row = lambda i: (i, 0)
    consts = [g, win, expand, qag, kvag, wuq, wukv] + list(tabs)
    return pl.pallas_call(
        _mla_prep_kernel,
        grid=(n // tm,),
        in_specs=[pl.BlockSpec((tm, d), row), pl.BlockSpec((tm // ROPE_BLOCK * ROPE_GROUP, 2 * LANES), row)]
                 + [_const_spec(c.shape) for c in consts],
        out_specs=[pl.BlockSpec((tm, MLA_HEADS * MLA_PAD), row), pl.BlockSpec((tm, MLA_HEADS * MLA_PAD), row),
                   pl.BlockSpec((tm, MLA_HEADS * MLA_PAD), row), pl.BlockSpec((tm, MEM_W), row)],
        out_shape=[jax.ShapeDtypeStruct((n, MLA_HEADS * MLA_PAD), BF16),
                   jax.ShapeDtypeStruct((n, MLA_HEADS * MLA_PAD), BF16),
                   jax.ShapeDtypeStruct((n, MLA_HEADS * MLA_PAD), BF16),
                   jax.ShapeDtypeStruct((n, MEM_W), F32)],
        compiler_params=_params(("arbitrary",)),
    )(x2, cs, *consts)


def _flash_kernel(qa_ref, qb_ref, k_ref, v_ref, o_ref):
    tq = qa_ref.shape[0]
    n_tiles = k_ref.shape[0] // tq
    r = lax.broadcasted_iota(jnp.int32, (tq, tq), 0)
    c = lax.broadcasted_iota(jnp.int32, (tq, tq), 1)
    heads = range(FLASH_HEADS)
    kc = [slice(h * MLA_PAD, (h + 1) * MLA_PAD) for h in heads]
    vc = [slice(h * MLA_VDIM, (h + 1) * MLA_VDIM) for h in heads]

    def attend(tiles):
        chains = [(q_ref, qi, h) for q_ref, qi in tiles for h in heads]
        sd = [_dot_nt(q_ref[:, kc[h]], k_ref[qi * tq:(qi + 1) * tq, kc[h]]) for q_ref, qi, h in chains]
        sp = [_dot_nt(q_ref[:, kc[h]], k_ref[0:qi * tq, kc[h]]) if qi > 0 else None for q_ref, qi, h in chains]
        pd, pp = [], []
        for n, (q_ref, qi, h) in enumerate(chains):
            s = jnp.where(c <= r, sd[n], -jnp.inf)
            m = jnp.max(s, axis=-1, keepdims=True)
            if qi > 0:
                m = jnp.maximum(m, jnp.max(sp[n], axis=-1, keepdims=True))
                pp.append(jnp.exp2((sp[n] - m).astype(BF16)))
            else:
                pp.append(None)
            pd.append(jnp.exp2((s - m).astype(BF16)))
        for n, (q_ref, qi, h) in enumerate(chains):
            acc = _dot(pd[n], v_ref[qi * tq:(qi + 1) * tq, kc[h]])
            if qi > 0:
                acc = acc + _dot(pp[n], v_ref[0:qi * tq, kc[h]])
            o_ref[qi * tq:(qi + 1) * tq, vc[h]] = (acc[:, 0:MLA_VDIM] / acc[:, MLA_VDIM:]).astype(BF16)

    for step in range((n_tiles + 1) // 2):
        late = n_tiles - 1 - step
        tiles = [(qa_ref, step)] + ([(qb_ref, late)] if late != step else [])
        pl.when(pl.program_id(2) == step)(functools.partial(attend, tiles))


def _flash(q, k, v, batch, seq, tq):
    q3 = q.reshape(batch, seq, MLA_HEADS * MLA_PAD)
    k3 = k.reshape(batch, seq, MLA_HEADS * MLA_PAD)
    v3 = v.reshape(batch, seq, MLA_HEADS * MLA_PAD)
    wq = FLASH_HEADS * MLA_PAD
    wv = FLASH_HEADS * MLA_VDIM
    n_tiles = seq // tq
    out = pl.pallas_call(
        _flash_kernel,
        grid=(batch, MLA_HEADS // FLASH_HEADS, (n_tiles + 1) // 2),
        in_specs=[pl.BlockSpec((None, tq, wq), lambda b, h, i: (b, i, h)),
                  pl.BlockSpec((None, tq, wq), lambda b, h, i: (b, n_tiles - 1 - i, h)),
                  pl.BlockSpec((None, seq, wq), lambda b, h, i: (b, 0, h)),
                  pl.BlockSpec((None, seq, wq), lambda b, h, i: (b, 0, h))],
        out_specs=pl.BlockSpec((None, seq, wv), lambda b, h, i: (b, 0, h)),
        out_shape=jax.ShapeDtypeStruct((batch, seq, MLA_HEADS * MLA_VDIM), BF16),
        compiler_params=_params(("arbitrary", "arbitrary", "arbitrary")),
    )(q3, q3, k3, v3)
    return out.reshape(batch * seq, MLA_HEADS * MLA_VDIM)


def _mla_layouts(w_in, w_uq, q_norm_g, k_norm_g):
    half = MLA_ROPE // 2
    c0 = MLA_Q_LORA + MLA_KV_LORA
    kpe = w_in[:, c0:c0 + MLA_ROPE]
    win = jnp.concatenate([w_in[:, :c0], kpe, kpe, w_in[:, c0 + MLA_ROPE:]], axis=1)
    wq = w_uq.reshape(MLA_Q_LORA, MLA_HEADS, MLA_QK)
    x1 = wq[:, :, MLA_NOPE:MLA_NOPE + half]
    x2 = wq[:, :, MLA_NOPE + half:]
    wuq = jnp.concatenate([wq[:, :, :MLA_NOPE], x1, x2, x2, x1], axis=2).reshape(MLA_Q_LORA, MLA_HEADS * MLA_PAD)
    g1, g2 = q_norm_g[MLA_NOPE:MLA_NOPE + half], q_norm_g[MLA_NOPE + half:]
    qgn = q_norm_g[:MLA_NOPE][None, :]
    qgr = jnp.concatenate([g1, g2, -g2, g1])[None, :]
    kgn = k_norm_g[:MLA_NOPE][None, :]
    kgr = jnp.concatenate([k_norm_g[MLA_NOPE:], k_norm_g[MLA_NOPE:]])[None, :]
    return win, wuq, (qgn, qgr, kgn, kgr)


def kernel(x, mem, positions, mix_norm_g, ffn_norm_g, mem_norm_g, w_mem_kv, mem_q_norm_g, mem_k_norm_g, w_out, w_up, conv_w, conv_b, w_down, hg_w_in, hg_lb_logits, hg_out_norm_g, mla_w_in, mla_qa_norm_g, mla_kva_norm_g, mla_w_uq, mla_w_ukv, mla_q_norm_g, mla_k_norm_g):
    batch, seq, d = x.shape
    n = batch * seq
    x2 = x.reshape(n, d)

    grp = jnp.asarray(np.kron(np.eye(MEM_HEADS), np.full((MEM_HDIM, MEM_HDIM), 1.0 / MEM_HDIM)), BF16)
    sums, tri, lev = _hg_tables()
    hg_tables = (jnp.asarray(sums, BF16), jnp.asarray(tri, BF16), jnp.asarray(lev))

    kmem, vmem = _memkv(mem, mem_norm_g[:, None, :], w_mem_kv.astype(BF16),
                        jnp.tile(mem_k_norm_g, (1, MEM_HEADS))[:, None, :], grp)
    qg = jnp.tile(mem_q_norm_g, (1, MEM_HEADS))[:, None, :]
    w_out_b, w_up_b, w_down_b = w_out.astype(BF16), w_up.astype(BF16), w_down.astype(BF16)

    def finish_layer(xin, y, qm, layer):
        return _layer_tail(xin, y, qm, kmem, vmem, layer, qg, grp, w_out_b, ffn_norm_g[:, None, :],
                           w_up_b, conv_w, conv_b[:, None, :], w_down_b, batch, seq, min(TAIL_ROWS, seq))

    y, qm = _hg_layer(x2, mix_norm_g[0][None, :], hg_w_in[0].astype(BF16), hg_lb_logits, hg_tables,
                      hg_out_norm_g[0][None, :], batch, seq)
    x2 = finish_layer(x2, y, qm, 0)

    win, wuq, tabs = _mla_layouts(mla_w_in[0], mla_w_uq[0], mla_q_norm_g[0], mla_k_norm_g[0])
    q, k, v, qm = _mla_prep(x2, _rope_tables(positions), mix_norm_g[1][None, :], win.astype(BF16),
                            jnp.asarray(_rope_expand(), BF16),
                            mla_qa_norm_g[0][None, :], mla_kva_norm_g[0][None, :], wuq.astype(BF16),
                            mla_w_ukv[0].astype(BF16), tabs, min(PREP_ROWS, seq))
    y = _flash(q, k, v, batch, seq, min(FLASH_TQ, seq))
    x2 = finish_layer(x2, y, qm, 1)
    return x2.reshape(batch, seq, d)
```
